```python
import jax, jax.numpy as jnp
from jax import lax
import numpy as np

D_MODEL = 1024
BATCH = 2
SEQ = 16384
DEPTH = 1
DEC_BATCH = 32
DEC_SEQ = 64
PAST_LEN = 4096

CHUNK = 64
GMLP_CHUNK = 128
GMLP_GROUPS = 8
GMLP_WIDTH = 1024
GMLP_GROUP_DIM = GMLP_WIDTH // GMLP_GROUPS
MLA_HEADS = 8
Q_LORA = 512
KV_LORA = 512
NOPE_DIM = 128
ROPE_DIM = 64
V_DIM = 128
QK_DIM = NOPE_DIM + ROPE_DIM
ROPE_BASE = 10000.0
ATTN_SCALE = QK_DIM ** -0.5
Q_BLOCK = 128
NEG_INF = -1e30
D_FF = 2816
CONV_W = 3
EPS = 1e-6
OFF_Q = 2 * GMLP_WIDTH
OFF_KV = OFF_Q + Q_LORA
OFF_GATE = OFF_KV + KV_LORA + ROPE_DIM
IN_COLS = OFF_GATE + 2 * D_MODEL

kernel_name = "hybrid_gmlp_mla_convffn_stream_step"


def _rmsnorm(x, g):
    xf = x.astype(jnp.float32)
    y = xf * lax.rsqrt(jnp.mean(xf * xf, axis=-1, keepdims=True) + EPS)
    return (y * g.astype(jnp.float32)).astype(x.dtype)


def _layernorm(x, g, b):
    xf = x.astype(jnp.float32)
    xc = xf - jnp.mean(xf, axis=-1, keepdims=True)
    var = jnp.mean(xc * xc, axis=-1, keepdims=True)
    return (xc * lax.rsqrt(var + EPS) * g.astype(jnp.float32) + b.astype(jnp.float32)).astype(x.dtype)


def _rope(x, pos):
    half = ROPE_DIM // 2
    inv_freq = ROPE_BASE ** (-jnp.arange(half, dtype=jnp.float32) / half)
    ang = pos.astype(jnp.float32)[:, None] * inv_freq[None, :]
    ang = ang.reshape(ang.shape[:1] + (1,) * (x.ndim - 3) + (half,))
    cos, sin = jnp.cos(ang), jnp.sin(ang)
    xf = x.astype(jnp.float32)
    x1, x2 = xf[..., :half], xf[..., half:]
    return jnp.concatenate([x1 * cos - x2 * sin, x1 * sin + x2 * cos], axis=-1).astype(x.dtype)


def _gmlp_branch(uv, ln_g, ln_b, w_s, b_s):
    u, v = jnp.split(jax.nn.gelu(uv), 2, axis=-1)
    v = _layernorm(v, ln_g, ln_b)
    bsz, L, _ = v.shape
    n = min(L, GMLP_CHUNK)
    causal = jnp.tril(jnp.ones((n, n), dtype=bool))
    w = jnp.where(causal, w_s[:, :n, :n], 0)
    vc = v.reshape(bsz, L // n, n, GMLP_GROUPS, GMLP_GROUP_DIM)
    s = jnp.einsum('gts,bnsgc->bntgc', w, vc) + b_s[:, :n].T[:, :, None]
    return u * s.reshape(bsz, L, GMLP_WIDTH), v


def _mla_qkv(q_lat, kv_lat, pos, q_norm_g, w_uq, kv_norm_g):
    q = jnp.einsum('bsr,rhd->bshd', _rmsnorm(q_lat, q_norm_g), w_uq)
    q = jnp.concatenate([q[..., :NOPE_DIM], _rope(q[..., NOPE_DIM:], pos)], axis=-1)
    c_kv = _rmsnorm(kv_lat[..., :KV_LORA], kv_norm_g)
    k_rope = _rope(kv_lat[..., KV_LORA:], pos)
    return q, c_kv, k_rope


def _expand_kv(c_kv, k_rope, w_uk, w_uv):
    k_nope = jnp.einsum('bsc,chd->bshd', c_kv, w_uk)
    k_r = jnp.broadcast_to(k_rope[:, :, None, :], k_nope.shape[:3] + (ROPE_DIM,))
    v = jnp.einsum('bsc,chd->bshd', c_kv, w_uv)
    return jnp.concatenate([k_nope, k_r], axis=-1), v


def _attend(q, q_pos, k, v, k_pos):
    s = jnp.einsum('bqhd,bkhd->bhqk', q, k).astype(jnp.float32) * ATTN_SCALE
    visible = (k_pos[None, :] // CHUNK) <= (q_pos[:, None] // CHUNK)
    s = jnp.where(visible, s, NEG_INF)
    p = jax.nn.softmax(s, axis=-1).astype(v.dtype)
    return jnp.einsum('bhqk,bkhd->bqhd', p, v)


def _conv_ffn(h, prev, w_up, conv_w, conv_b, w_down):
    up = h @ w_up
    L = up.shape[1]
    full = jnp.concatenate([prev, up], axis=1)
    c = conv_b + sum(full[:, i:i + L] * conv_w[i] for i in range(CONV_W))
    val, gate = jnp.split(c, 2, axis=-1)
    return (jax.nn.silu(gate) * val) @ w_down, full[:, L:]


def _layer(x, pos, past_ckv, past_krope, past_conv,
           norm_mix_g, w_in, gmlp_ln_g, gmlp_ln_b, gmlp_w_s, gmlp_b_s,
           mla_q_norm_g, mla_w_uq, mla_kv_norm_g, mla_w_uk, mla_w_uv,
           w_proj_a, w_proj_b, w_out, norm_ffn_g, ffn_w_up, ffn_conv_w, ffn_conv_b, ffn_w_down):
    bsz, L, _ = x.shape
    z = _rmsnorm(x, norm_mix_g) @ w_in
    gates = jax.nn.sigmoid(z[..., OFF_GATE:])
    a_out, v_rows = _gmlp_branch(z[..., :OFF_Q], gmlp_ln_g, gmlp_ln_b, gmlp_w_s, gmlp_b_s)
    q, c_kv, k_rope = _mla_qkv(z[..., OFF_Q:OFF_KV], z[..., OFF_KV:OFF_GATE], pos,
                               mla_q_norm_g, mla_w_uq, mla_kv_norm_g)
    if past_ckv is None:
        k, v = _expand_kv(c_kv, k_rope, mla_w_uk, mla_w_uv)
        nb = L // Q_BLOCK
        qb = jnp.moveaxis(q.reshape(bsz, nb, Q_BLOCK, MLA_HEADS, QK_DIM), 1, 0)
        pb = pos.reshape(nb, Q_BLOCK)
        ob = lax.map(lambda qp: _attend(qp[0], qp[1], k, v, pos), (qb, pb))
        attn = jnp.moveaxis(ob, 0, 1).reshape(bsz, L, MLA_HEADS * V_DIM)
    else:
        k, v = _expand_kv(jnp.concatenate([past_ckv, c_kv], axis=1),
                          jnp.concatenate([past_krope, k_rope], axis=1), mla_w_uk, mla_w_uv)
        k_pos = jnp.arange(past_ckv.shape[1] + L, dtype=jnp.int32)
        attn = _attend(q, pos, k, v, k_pos).reshape(bsz, L, MLA_HEADS * V_DIM)
    merged = gates[..., :D_MODEL] * (a_out @ w_proj_a) + gates[..., D_MODEL:] * (attn @ w_proj_b)
    h = x + merged @ w_out
    ffn_out, conv_rows = _conv_ffn(_rmsnorm(h, norm_ffn_g), past_conv,
                                   ffn_w_up, ffn_conv_w, ffn_conv_b, ffn_w_down)
    return h + ffn_out, c_kv, k_rope, conv_rows, v_rows


def setup_inputs(seed: int = 0) -> dict:
    key = jax.random.key(seed)
    ks = jax.random.split(key, 32)

    def nrm(k, shape, scale=1.0):
        return jax.random.normal(k, shape, jnp.float32) * scale

    return {
        "x_prompt": nrm(ks[0], (BATCH, SEQ, D_MODEL)),
        "x_sample": nrm(ks[1], (DEC_BATCH, DEC_SEQ, D_MODEL)),
        "cache_mla_ckv": nrm(ks[2], (DEPTH, DEC_BATCH, PAST_LEN, KV_LORA)),
        "cache_mla_krope": nrm(ks[3], (DEPTH, DEC_BATCH, PAST_LEN, ROPE_DIM)),
        "state_ffn_conv": nrm(ks[4], (DEPTH, DEC_BATCH, CONV_W - 1, 2 * D_FF)),
        "norm_mix_g": 1.0 + nrm(ks[5], (DEPTH, D_MODEL), 0.02),
        "w_in": nrm(ks[6], (DEPTH, D_MODEL, IN_COLS), D_MODEL ** -0.5),
        "gmlp_ln_g": 1.0 + nrm(ks[7], (DEPTH, GMLP_WIDTH), 0.02),
        "gmlp_ln_b": nrm(ks[8], (DEPTH, GMLP_WIDTH), 0.02),
        "gmlp_w_s": nrm(ks[9], (DEPTH, GMLP_GROUPS, GMLP_CHUNK, GMLP_CHUNK), GMLP_CHUNK ** -0.5),
        "gmlp_b_s": 1.0 + nrm(ks[10], (DEPTH, GMLP_GROUPS, GMLP_CHUNK), 0.02),
        "mla_q_norm_g": 1.0 + nrm(ks[11], (DEPTH, Q_LORA), 0.02),
        "mla_w_uq": nrm(ks[12], (DEPTH, Q_LORA, MLA_HEADS, QK_DIM), Q_LORA ** -0.5),
        "mla_kv_norm_g": 1.0 + nrm(ks[13], (DEPTH, KV_LORA), 0.02),
        "mla_w_uk": nrm(ks[14], (DEPTH, KV_LORA, MLA_HEADS, NOPE_DIM), KV_LORA ** -0.5),
        "mla_w_uv": nrm(ks[15], (DEPTH, KV_LORA, MLA_HEADS, V_DIM), KV_LORA ** -0.5),
        "w_proj_a": nrm(ks[16], (DEPTH, GMLP_WIDTH, D_MODEL), GMLP_WIDTH ** -0.5),
        "w_proj_b": nrm(ks[17], (DEPTH, MLA_HEADS * V_DIM, D_MODEL), (MLA_HEADS * V_DIM) ** -0.5),
        "w_out": nrm(ks[18], (DEPTH, D_MODEL, D_MODEL), D_MODEL ** -0.5),
        "norm_ffn_g": 1.0 + nrm(ks[19], (DEPTH, D_MODEL), 0.02),
        "ffn_w_up": nrm(ks[20], (DEPTH, D_MODEL, 2 * D_FF), D_MODEL ** -0.5),
        "ffn_conv_w": nrm(ks[21], (DEPTH, CONV_W, 2 * D_FF), CONV_W ** -0.5),
        "ffn_conv_b": nrm(ks[22], (DEPTH, 2 * D_FF), 0.02),
        "ffn_w_down": nrm(ks[23], (DEPTH, D_FF, D_MODEL), D_FF ** -0.5),
        "final_norm_g": 1.0 + nrm(ks[24], (D_MODEL,), 0.02),
    }


def reference(x_prompt, x_sample, cache_mla_ckv, cache_mla_krope, state_ffn_conv,
              norm_mix_g, w_in, gmlp_ln_g, gmlp_ln_b, gmlp_w_s, gmlp_b_s,
              mla_q_norm_g, mla_w_uq, mla_kv_norm_g, mla_w_uk, mla_w_uv,
              w_proj_a, w_proj_b, w_out, norm_ffn_g, ffn_w_up, ffn_conv_w, ffn_conv_b, ffn_w_down,
              final_norm_g):
    seq = x_prompt.shape[1]
    dec_seq = x_sample.shape[1]
    past_len = cache_mla_ckv.shape[2]
    pos_p = jnp.arange(seq, dtype=jnp.int32)
    pos_s = past_len + jnp.arange(dec_seq, dtype=jnp.int32)
    conv_zero = jnp.zeros((x_prompt.shape[0], CONV_W - 1, 2 * D_FF), x_prompt.dtype)

    y_p, y_s = x_prompt, x_sample
    ckv_p, kr_p, cv_p, ckv_s, kr_s, cv_s, gv_s = [], [], [], [], [], [], []
    for l in range(DEPTH):
        w = (norm_mix_g[l], w_in[l], gmlp_ln_g[l], gmlp_ln_b[l], gmlp_w_s[l], gmlp_b_s[l],
             mla_q_norm_g[l], mla_w_uq[l], mla_kv_norm_g[l], mla_w_uk[l], mla_w_uv[l],
             w_proj_a[l], w_proj_b[l], w_out[l], norm_ffn_g[l], ffn_w_up[l], ffn_conv_w[l],
             ffn_conv_b[l], ffn_w_down[l])
        y_p, c1, k1, v1, _ = _layer(y_p, pos_p, None, None, conv_zero, *w)
        y_s, c2, k2, v2, g2 = _layer(y_s, pos_s, cache_mla_ckv[l], cache_mla_krope[l],
                                     state_ffn_conv[l], *w)
        ckv_p.append(c1); kr_p.append(k1); cv_p.append(v1)
        ckv_s.append(c2); kr_s.append(k2); cv_s.append(v2); gv_s.append(g2)

    y_prompt = _rmsnorm(y_p, final_norm_g)
    y_sample = _rmsnorm(y_s, final_norm_g)
    return (y_prompt, y_sample,
            jnp.stack(ckv_p), jnp.stack(kr_p), jnp.stack(cv_p),
            jnp.stack(ckv_s), jnp.stack(kr_s), jnp.stack(cv_s), jnp.stack(gv_s))
```

```python
import functools

import numpy as np
import jax
import jax.numpy as jnp
from jax import lax
from jax.experimental import pallas as pl
from jax.experimental.pallas import tpu as pltpu

D_MODEL = 1024
CHUNK = 64
GMLP_CHUNK = 128
GMLP_GROUPS = 8
GMLP_WIDTH = 1024
GMLP_GROUP_DIM = GMLP_WIDTH // GMLP_GROUPS
MLA_HEADS = 8
Q_LORA = 512
KV_LORA = 512
NOPE_DIM = 128
ROPE_DIM = 64
V_DIM = 128
QK_DIM = NOPE_DIM + ROPE_DIM
ROPE_BASE = 10000.0
ATTN_SCALE = QK_DIM ** -0.5
NEG_INF = -1e30
D_FF = 2816
CONV_W = 3
EPS = 1e-6
OFF_Q = 2 * GMLP_WIDTH
OFF_KV = OFF_Q + Q_LORA
OFF_GATE = OFF_KV + KV_LORA + ROPE_DIM

LANES = 128
HALF_ROPE = ROPE_DIM // 2
FF_CHUNK = 256
N_FF_CHUNKS = D_FF // FF_CHUNK
VMEM_LIMIT = 56 * 1024 * 1024
PREP_ROWS = 256
FLASH_ROWS = 512
FFN_ROWS = 512

BF16 = jnp.bfloat16
F32 = jnp.float32


def _rms(x, g):
    return x * lax.rsqrt(jnp.mean(x * x, axis=-1, keepdims=True) + EPS) * g


def _dot(a, b):
    return jnp.dot(a, b, preferred_element_type=F32)


def _dot_nt(a, b):
    return lax.dot_general(a, b, (((1,), (1,)), ((), ())), preferred_element_type=F32)


def _resident(shape):
    nd = len(shape)
    return pl.BlockSpec(shape, lambda *_: (0,) * nd, pipeline_mode=pl.Buffered(1))


def _gmlp_gate_kernel(x_ref, g_ref, wuv_ref, wgate_ref, lng_ref, lnb_ref, ws_ref, bs_ref, wa_ref,
                      ga_ref, gb_ref, *maybe_v_ref, seg):
    tm = x_ref.shape[0]
    xn = _rms(x_ref[...], g_ref[...]).astype(BF16)
    ge = jax.nn.gelu(_dot(xn, wuv_ref[...]))
    u = ge[:, :GMLP_WIDTH]
    v = ge[:, GMLP_WIDTH:]
    vc = v - jnp.mean(v, axis=-1, keepdims=True)
    var = jnp.mean(vc * vc, axis=-1, keepdims=True)
    v = vc * lax.rsqrt(var + EPS) * lng_ref[...] + lnb_ref[...]
    if maybe_v_ref:
        maybe_v_ref[0][...] = v
    vb = v.astype(BF16)

    row = lax.broadcasted_iota(jnp.int32, (GMLP_CHUNK, GMLP_CHUNK), 0)
    col = lax.broadcasted_iota(jnp.int32, (GMLP_CHUNK, GMLP_CHUNK), 1)
    keep = (col <= row) & ((col // seg) == (row // seg))
    wm = [jnp.where(keep, ws_ref[g], 0.0).astype(BF16) for g in range(GMLP_GROUPS)]
    bias = bs_ref[...]
    blocks = []
    for c in range(tm // GMLP_CHUNK):
        rows = slice(c * GMLP_CHUNK, (c + 1) * GMLP_CHUNK)
        mix = [_dot(wm[g], vb[rows, g * GMLP_GROUP_DIM:(g + 1) * GMLP_GROUP_DIM])
               for g in range(GMLP_GROUPS)]
        blocks.append(u[rows] * (jnp.concatenate(mix, axis=1) + bias))
    a_out = jnp.concatenate(blocks, axis=0).astype(BF16)
    pa = _dot(a_out, wa_ref[...])
    gates = jax.nn.sigmoid(_dot(xn, wgate_ref[...]))
    ga_ref[...] = (gates[:, :D_MODEL] * pa).astype(BF16)
    gb_ref[...] = gates[:, D_MODEL:].astype(BF16)


def _gmlp_gate(x2d, norm_g, wuv, wgate, ln_g, ln_b, ws, bias_full, wa, *, seg, want_v, tm):
    n = x2d.shape[0]
    row_spec = pl.BlockSpec((tm, D_MODEL), lambda i: (i, 0))
    out_shape = [jax.ShapeDtypeStruct((n, D_MODEL), BF16), jax.ShapeDtypeStruct((n, D_MODEL), BF16)]
    out_specs = [row_spec, row_spec]
    if want_v:
        out_shape.append(jax.ShapeDtypeStruct((n, GMLP_WIDTH), F32))
        out_specs.append(row_spec)
    return pl.pallas_call(
        functools.partial(_gmlp_gate_kernel, seg=seg),
        grid=(n // tm,),
        in_specs=[row_spec, _resident(norm_g.shape), _resident(wuv.shape), _resident(wgate.shape),
                  _resident(ln_g.shape), _resident(ln_b.shape), _resident(ws.shape),
                  _resident(bias_full.shape), _resident(wa.shape)],
        out_specs=out_specs,
        out_shape=out_shape,
        compiler_params=pltpu.CompilerParams(dimension_semantics=("arbitrary",),
                                             vmem_limit_bytes=VMEM_LIMIT),
        name="gmlp_gate",
    )(x2d, norm_g, wuv, wgate, ln_g, ln_b, ws, bias_full, wa)


def _mla_prep_kernel(x_ref, g_ref, wmla_ref, qg_ref, wuq_ref, kvg_ref, cos_ref, sin_ref, *rest, expand):
    if expand:
        wuk_ref, wuv_ref, qn_ref, qr_ref, ckv_ref, kro_ref, kn_ref, kr2_ref, vv_ref = rest
    else:
        qn_ref, qr_ref, ckv_ref, kro_ref, kdup_ref = rest
    xn = _rms(x_ref[...], g_ref[...]).astype(BF16)
    z = _dot(xn, wmla_ref[...])
    q_lat = z[:, :Q_LORA]
    ckv_raw = z[:, Q_LORA:Q_LORA + KV_LORA]
    k_main = z[:, Q_LORA + KV_LORA:Q_LORA + KV_LORA + LANES]
    k_rot = z[:, Q_LORA + KV_LORA + LANES:]
    cos = cos_ref[...]
    sin = sin_ref[...]

    q = _dot(_rms(q_lat, qg_ref[...]).astype(BF16), wuq_ref[...])
    n_nope = MLA_HEADS * NOPE_DIM
    n_rope = MLA_HEADS * ROPE_DIM
    reps = n_rope // LANES
    cos_q = jnp.concatenate([cos] * reps, axis=1)
    sin_q = jnp.concatenate([sin] * reps, axis=1)
    qn_ref[...] = (q[:, :n_nope] * ATTN_SCALE).astype(BF16)
    q_rope = q[:, n_nope:n_nope + n_rope] * cos_q + q[:, n_nope + n_rope:] * sin_q
    qr_ref[...] = (q_rope * ATTN_SCALE).astype(BF16)

    c_kv = _rms(ckv_raw, kvg_ref[...])
    ckv_ref[...] = c_kv
    k_dup = k_main * cos + k_rot * sin
    kro_ref[...] = k_dup[:, :ROPE_DIM]
    if expand:
        cb = c_kv.astype(BF16)
        kn_ref[...] = _dot(cb, wuk_ref[...]).astype(BF16)
        vv_ref[...] = _dot(cb, wuv_ref[...]).astype(BF16)
        lane = lax.broadcasted_iota(jnp.int32, k_dup.shape, 1)
        kr2_ref[:, :LANES] = jnp.where(lane < ROPE_DIM, k_dup, 0.0).astype(BF16)
        kr2_ref[:, LANES:] = jnp.where(lane >= ROPE_DIM, k_dup, 0.0).astype(BF16)
    else:
        kdup_ref[...] = k_dup.astype(BF16)


def _mla_prep(x2d, norm_g, wmla, q_g, wuq, kv_g, cos_t, sin_t, wuk, wuv, *, expand, tm):
    n = x2d.shape[0]
    table_blocks = cos_t.shape[0] // tm

    def rows(width):
        return pl.BlockSpec((tm, width), lambda i: (i, 0))

    table_spec = pl.BlockSpec((tm, LANES), lambda i: (i % table_blocks, 0))

    n_nope = MLA_HEADS * NOPE_DIM
    n_rope = MLA_HEADS * ROPE_DIM
    in_arrays = [x2d, norm_g, wmla, q_g, wuq, kv_g, cos_t, sin_t]
    in_specs = [rows(D_MODEL), _resident(norm_g.shape), _resident(wmla.shape), _resident(q_g.shape),
                _resident(wuq.shape), _resident(kv_g.shape), table_spec, table_spec]
    out_shape = [jax.ShapeDtypeStruct((n, n_nope), BF16), jax.ShapeDtypeStruct((n, n_rope), BF16),
                 jax.ShapeDtypeStruct((n, KV_LORA), F32), jax.ShapeDtypeStruct((n, ROPE_DIM), F32)]
    out_specs = [rows(n_nope), rows(n_rope), rows(KV_LORA), rows(ROPE_DIM)]
    if expand:
        in_arrays += [wuk, wuv]
        in_specs += [_resident(wuk.shape), _resident(wuv.shape)]
        out_shape += [jax.ShapeDtypeStruct((n, n_nope), BF16), jax.ShapeDtypeStruct((n, 2 * LANES), BF16),
                      jax.ShapeDtypeStruct((n, MLA_HEADS * V_DIM), BF16)]
        out_specs += [rows(n_nope), rows(2 * LANES), rows(MLA_HEADS * V_DIM)]
    else:
        out_shape += [jax.ShapeDtypeStruct((n, LANES), BF16)]
        out_specs += [rows(LANES)]
    return pl.pallas_call(
        functools.partial(_mla_prep_kernel, expand=expand),
        grid=(n // tm,),
        in_specs=in_specs,
        out_specs=out_specs,
        out_shape=out_shape,
        compiler_params=pltpu.CompilerParams(dimension_semantics=("arbitrary",),
                                             vmem_limit_bytes=VMEM_LIMIT),
        name="mla_prep",
    )(*in_arrays)


def _flash_kernel(qn_ref, qr_ref, kn_ref, kr_ref, v_ref, o_ref, m_ref, l_ref, acc_ref, *, t):
    qi = pl.program_id(2)
    q = jnp.concatenate([qn_ref[0], qr_ref[0]], axis=1)
    m_ref[...] = jnp.full(m_ref.shape, -jnp.inf, F32)
    l_ref[...] = jnp.zeros(l_ref.shape, F32)
    acc_ref[...] = jnp.zeros(acc_ref.shape, F32)

    def step(kb, masked):
        start = pl.multiple_of(kb * t, t)
        k = jnp.concatenate([kn_ref[0, pl.ds(start, t), :], kr_ref[0, pl.ds(start, t), :]], axis=1)
        s = _dot_nt(q, k)
        if masked:
            row = lax.broadcasted_iota(jnp.int32, s.shape, 0)
            col = lax.broadcasted_iota(jnp.int32, s.shape, 1)
            s = jnp.where((col // CHUNK) <= (row // CHUNK), s, NEG_INF)
        m_old = m_ref[...]
        m_new = jnp.maximum(m_old, jnp.max(s, axis=1, keepdims=True))
        alpha = jnp.exp(m_old - m_new)
        p = jnp.exp(s - m_new)
        l_ref[...] = alpha * l_ref[...] + jnp.sum(p, axis=1, keepdims=True)
        acc_ref[...] = alpha * acc_ref[...] + _dot(p.astype(BF16), v_ref[0, pl.ds(start, t), :])
        m_ref[...] = m_new

    def body(kb, carry):
        step(kb, False)
        return carry

    lax.fori_loop(0, qi, body, 0)
    step(qi, True)
    o_ref[0] = (acc_ref[...] / l_ref[...]).astype(o_ref.dtype)


def _flash_attention(qn, qr, kn, kr2, vv, *, t):
    b, s, _ = qn.shape
    return pl.pallas_call(
        functools.partial(_flash_kernel, t=t),
        grid=(b, MLA_HEADS, s // t),
        in_specs=[
            pl.BlockSpec((1, t, LANES), lambda bi, h, qi: (bi, qi, h)),
            pl.BlockSpec((1, t, LANES), lambda bi, h, qi: (bi, qi, h // 2)),
            pl.BlockSpec((1, s, LANES), lambda bi, h, qi: (bi, 0, h)),
            pl.BlockSpec((1, s, LANES), lambda bi, h, qi: (bi, 0, h % 2)),
            pl.BlockSpec((1, s, LANES), lambda bi, h, qi: (bi, 0, h)),
        ],
        out_specs=pl.BlockSpec((1, t, LANES), lambda bi, h, qi: (bi, qi, h)),
        out_shape=jax.ShapeDtypeStruct((b, s, MLA_HEADS * V_DIM), BF16),
        scratch_shapes=[pltpu.VMEM((t, 1), F32), pltpu.VMEM((t, 1), F32), pltpu.VMEM((t, V_DIM), F32)],
        compiler_params=pltpu.CompilerParams(dimension_semantics=("arbitrary", "arbitrary", "arbitrary"),
                                             vmem_limit_bytes=VMEM_LIMIT),
        name="flash_attention",
    )(qn, qr, kn, kr2, vv)


def _sample_attn_kernel(qn_ref, qr_ref, ckvn_ref, kdup_ref, cache_ref, krc_ref, wuk_ref, wuv_ref, o_ref,
                        *, past, n_new, need_mask):
    qn = qn_ref[0]
    qr = qr_ref[0]
    lane = lax.broadcasted_iota(jnp.int32, (n_new, LANES), 1)
    q_lat, q_rope = [], []
    for h in range(MLA_HEADS):
        q_lat.append(_dot_nt(qn[:, h * NOPE_DIM:(h + 1) * NOPE_DIM],
                             wuk_ref[:, h * NOPE_DIM:(h + 1) * NOPE_DIM]).astype(BF16))
        pair = qr[:, (h // 2) * LANES:(h // 2 + 1) * LANES]
        half = (lane < ROPE_DIM) if h % 2 == 0 else (lane >= ROPE_DIM)
        q_rope.append(jnp.where(half, pair, jnp.zeros_like(pair)))
    q_all = jnp.concatenate([jnp.concatenate(q_lat, axis=0), jnp.concatenate(q_rope, axis=0)], axis=1)

    k_cache = jnp.concatenate([cache_ref[0, 0].astype(BF16), krc_ref[0]], axis=1)
    pad = LANES - n_new
    c_new = jnp.concatenate([ckvn_ref[0].astype(BF16), jnp.zeros((pad, KV_LORA), BF16)], axis=0)
    r_new = jnp.concatenate([kdup_ref[0], jnp.zeros((pad, LANES), BF16)], axis=0)
    k_new = jnp.concatenate([c_new, r_new], axis=1)

    rows_total = MLA_HEADS * n_new
    half_rows = rows_total // 2
    heads_per_half = MLA_HEADS // 2
    outs = []
    for part in range(2):
        qp = q_all[part * half_rows:(part + 1) * half_rows]
        s_c = _dot_nt(qp, k_cache)
        s_n = _dot_nt(qp, k_new)
        col_n = lax.broadcasted_iota(jnp.int32, s_n.shape, 1)
        valid_n = col_n < n_new
        if need_mask:
            q_pos = past + lax.broadcasted_iota(jnp.int32, s_c.shape, 0) % n_new
            k_pos = lax.broadcasted_iota(jnp.int32, s_c.shape, 1)
            s_c = jnp.where((k_pos // CHUNK) <= (q_pos // CHUNK), s_c, NEG_INF)
            q_pos_n = past + lax.broadcasted_iota(jnp.int32, s_n.shape, 0) % n_new
            valid_n = valid_n & (((past + col_n) // CHUNK) <= (q_pos_n // CHUNK))
        m = jnp.maximum(jnp.max(s_c, axis=1, keepdims=True),
                        jnp.max(jnp.where(valid_n, s_n, NEG_INF), axis=1, keepdims=True))
        p_c = jnp.exp(s_c - m)
        p_n = jnp.where(col_n < n_new, jnp.exp(jnp.where(valid_n, s_n, NEG_INF) - m), 0.0)
        denom = jnp.sum(p_c, axis=1, keepdims=True) + jnp.sum(p_n, axis=1, keepdims=True)
        o_lat = _dot(p_c.astype(BF16), k_cache[:, :KV_LORA]) + _dot(p_n.astype(BF16), c_new)
        o_lat = (o_lat / denom).astype(BF16)
        for hh in range(heads_per_half):
            h = part * heads_per_half + hh
            outs.append(_dot(o_lat[hh * n_new:(hh + 1) * n_new],
                             wuv_ref[:, h * V_DIM:(h + 1) * V_DIM]))
    o_ref[0] = jnp.concatenate(outs, axis=1).astype(o_ref.dtype)


def _sample_attention(qn, qr, ckv_new, kdup, cache_ckv, krc, wuk, wuv, *, need_mask):
    nb, n_new, _ = qn.shape
    past = cache_ckv.shape[2]
    return pl.pallas_call(
        functools.partial(_sample_attn_kernel, past=past, n_new=n_new, need_mask=need_mask),
        grid=(nb,),
        in_specs=[
            pl.BlockSpec((1, n_new, qn.shape[2]), lambda i: (i, 0, 0)),
            pl.BlockSpec((1, n_new, qr.shape[2]), lambda i: (i, 0, 0)),
            pl.BlockSpec((1, n_new, KV_LORA), lambda i: (i, 0, 0)),
            pl.BlockSpec((1, n_new, LANES), lambda i: (i, 0, 0)),
            pl.BlockSpec((1, 1, past, KV_LORA), lambda i: (0, i, 0, 0)),
            pl.BlockSpec((1, past, LANES), lambda i: (i, 0, 0)),
            _resident(wuk.shape), _resident(wuv.shape),
        ],
        out_specs=pl.BlockSpec((1, n_new, MLA_HEADS * V_DIM), lambda i: (i, 0, 0)),
        out_shape=jax.ShapeDtypeStruct((nb, n_new, MLA_HEADS * V_DIM), BF16),
        compiler_params=pltpu.CompilerParams(dimension_semantics=("arbitrary",),
                                             vmem_limit_bytes=VMEM_LIMIT),
        name="sample_attention",
    )(qn, qr, ckv_new, kdup, cache_ckv, krc, wuk, wuv)


def _out_ffn_kernel(x_ref, ga_ref, gb_ref, attn_ref, prev_ref, wb_ref, wo_ref, fg_ref, wup_ref, cw_ref, cb_ref,
                    wdn_ref, fing_ref, y_ref, conv_ref, carry_ref, *, seg_rows, tiles_per_seq):
    tm = x_ref.shape[0]
    nseg = tm // seg_rows
    merged = ga_ref[...].astype(F32) + gb_ref[...].astype(F32) * _dot(attn_ref[...], wb_ref[...])
    h = x_ref[...] + _dot(merged.astype(BF16), wo_ref[...])
    hn = _rms(h, fg_ref[...]).astype(BF16)

    if tiles_per_seq == 1:
        prev = prev_ref[...]
    else:
        first = (pl.program_id(0) % tiles_per_seq) == 0
        prev = jnp.where(first, prev_ref[...], carry_ref[...][None])

    row = lax.broadcasted_iota(jnp.int32, (seg_rows, FF_CHUNK), 0)

    def conv(up, c0):
        cols = slice(c0, c0 + FF_CHUNK)
        w = cw_ref[:, cols]
        outs = []
        for s in range(nseg):
            u = up[s * seg_rows:(s + 1) * seg_rows]
            p0 = prev[s, 0:1, cols]
            p1 = prev[s, 1:2, cols]
            d1 = jnp.where(row == 0, p1, pltpu.roll(u, 1, 0))
            d2 = jnp.where(row == 0, p0, jnp.where(row == 1, p1, pltpu.roll(u, 2, 0)))
            outs.append(cb_ref[:, cols] + d2 * w[0:1] + d1 * w[1:2] + u * w[2:3])
            tail = u[seg_rows - (CONV_W - 1):]
            conv_ref[s, :, cols] = tail
            if tiles_per_seq > 1:
                carry_ref[:, cols] = tail
        return outs[0] if nseg == 1 else jnp.concatenate(outs, axis=0)

    acc = jnp.zeros((tm, D_MODEL), F32)
    for j in range(N_FF_CHUNKS):
        c_val = j * FF_CHUNK
        c_gate = D_FF + j * FF_CHUNK
        val = conv(_dot(hn, wup_ref[:, c_val:c_val + FF_CHUNK]), c_val)
        gate = conv(_dot(hn, wup_ref[:, c_gate:c_gate + FF_CHUNK]), c_gate)
        act = (jax.nn.silu(gate) * val).astype(BF16)
        acc = acc + _dot(act, wdn_ref[c_val:c_val + FF_CHUNK, :])
    y_ref[...] = _rms(h + acc, fing_ref[...])


def _out_ffn(x2d, ga, gb, attn, prev, wb, wo, ffn_g, wup, conv_w, conv_b, wdn, fin_g, *, seg_rows, tm):
    n = x2d.shape[0]
    nseg = tm // seg_rows if seg_rows < tm else 1
    seg = min(seg_rows, tm)
    tiles_per_seq = max(seg_rows // tm, 1)
    row_spec = pl.BlockSpec((tm, D_MODEL), lambda i: (i, 0))
    state_spec = pl.BlockSpec((nseg, CONV_W - 1, 2 * D_FF),
                              (lambda i: (i // tiles_per_seq, 0, 0)) if tiles_per_seq > 1 else (lambda i: (i, 0, 0)))
    return pl.pallas_call(
        functools.partial(_out_ffn_kernel, seg_rows=seg, tiles_per_seq=tiles_per_seq),
        grid=(n // tm,),
        in_specs=[row_spec, row_spec, row_spec, row_spec, state_spec,
                  _resident(wb.shape), _resident(wo.shape), _resident(ffn_g.shape), _resident(wup.shape),
                  _resident(conv_w.shape), _resident(conv_b.shape), _resident(wdn.shape),
                  _resident(fin_g.shape)],
        out_specs=[row_spec, state_spec],
        out_shape=[jax.ShapeDtypeStruct((n, D_MODEL), F32),
                   jax.ShapeDtypeStruct(prev.shape, F32)],
        scratch_shapes=[pltpu.VMEM((CONV_W - 1, 2 * D_FF), F32)],
        compiler_params=pltpu.CompilerParams(dimension_semantics=("arbitrary",),
                                             vmem_limit_bytes=VMEM_LIMIT),
        name="out_ffn",
    )(x2d, ga, gb, attn, prev, wb, wo, ffn_g, wup, conv_w, conv_b, wdn, fin_g)


def _rope_tables(pos):
    inv_freq = ROPE_BASE ** (-jnp.arange(HALF_ROPE, dtype=F32) / HALF_ROPE)
    ang = pos.astype(F32)[:, None] * inv_freq[None, :]
    reps = LANES // HALF_ROPE
    return jnp.tile(jnp.cos(ang), (1, reps)), jnp.tile(jnp.sin(ang), (1, reps))


def _rotated_cols(w):
    return jnp.concatenate([-w[..., HALF_ROPE:], w[..., :HALF_ROPE]], axis=-1)


def kernel(x_prompt, x_sample, cache_mla_ckv, cache_mla_krope, state_ffn_conv, norm_mix_g, w_in, gmlp_ln_g,
           gmlp_ln_b, gmlp_w_s, gmlp_b_s, mla_q_norm_g, mla_w_uq, mla_kv_norm_g, mla_w_uk, mla_w_uv,
           w_proj_a, w_proj_b, w_out, norm_ffn_g, ffn_w_up, ffn_conv_w, ffn_conv_b, ffn_w_down,
           final_norm_g):
    assert w_in.shape[0] == 1, "single-layer step"
    bp, sp, _ = x_prompt.shape
    bs, ss, _ = x_sample.shape
    past = cache_mla_ckv.shape[2]
    assert sp % GMLP_CHUNK == 0 and ss <= GMLP_CHUNK and GMLP_CHUNK % ss == 0

    w0 = w_in[0]
    wuv_in = w0[:, :OFF_Q].astype(BF16)
    wgate = w0[:, OFF_GATE:].astype(BF16)
    w_kr = w0[:, OFF_KV + KV_LORA:OFF_GATE]
    w_kr_rot = _rotated_cols(w_kr)
    wmla = jnp.concatenate([w0[:, OFF_Q:OFF_KV + KV_LORA], w_kr, w_kr, w_kr_rot, w_kr_rot], axis=1).astype(BF16)
    uq = mla_w_uq[0]
    uq_rope = uq[:, :, NOPE_DIM:]
    wuq = jnp.concatenate([uq[:, :, :NOPE_DIM].reshape(Q_LORA, -1), uq_rope.reshape(Q_LORA, -1),
                           _rotated_cols(uq_rope).reshape(Q_LORA, -1)], axis=1).astype(BF16)
    wuk = mla_w_uk[0].reshape(KV_LORA, -1).astype(BF16)
    wuv = mla_w_uv[0].reshape(KV_LORA, -1).astype(BF16)
    wa = w_proj_a[0].astype(BF16)
    wb = w_proj_b[0].astype(BF16)
    wo = w_out[0].astype(BF16)
    wup = ffn_w_up[0].astype(BF16)
    wdn = ffn_w_down[0].astype(BF16)
    conv_w = ffn_conv_w[0]
    conv_b = ffn_conv_b

    def spatial_params(n):
        reps = GMLP_CHUNK // n
        ws = jnp.tile(gmlp_w_s[0][:, :n, :n], (1, reps, reps))
        bias = jnp.tile(jnp.repeat(gmlp_b_s[0][:, :n].T, GMLP_GROUP_DIM, axis=1), (reps, 1))
        return ws, bias

    def run_stream(x, pos_row, n_spatial, prev_conv, *, is_prompt):
        b, s, _ = x.shape
        x2d = x.reshape(b * s, D_MODEL)
        ws, bias = spatial_params(n_spatial)
        res = _gmlp_gate(x2d, norm_mix_g, wuv_in, wgate, gmlp_ln_g, gmlp_ln_b, ws, bias, wa,
                         seg=n_spatial, want_v=not is_prompt, tm=PREP_ROWS)
        cos_t, sin_t = _rope_tables(pos_row)
        prep = _mla_prep(x2d, norm_mix_g, wmla, mla_q_norm_g, wuq, mla_kv_norm_g, cos_t, sin_t, wuk, wuv,
                         expand=is_prompt, tm=PREP_ROWS)
        if is_prompt:
            ga, gb = res
            qn, qr, ckv, kro, kn, kr2, vv = prep
            r3 = lambda a: a.reshape(b, s, a.shape[-1])
            attn = _flash_attention(r3(qn), r3(qr), r3(kn), r3(kr2), r3(vv), t=FLASH_ROWS)
        else:
            ga, gb, v_rows = res
            qn, qr, ckv, kro, kdup = prep
            r3 = lambda a: a.reshape(b, s, a.shape[-1])
            krc = jnp.concatenate([cache_mla_krope[0], cache_mla_krope[0]], axis=-1).astype(BF16)
            k_pos = np.arange(past + s)
            q_pos = past + np.arange(s)
            need_mask = not bool(np.all((k_pos[None, :] // CHUNK) <= (q_pos[:, None] // CHUNK)))
            attn = _sample_attention(r3(qn), r3(qr), r3(ckv), r3(kdup), cache_mla_ckv, krc, wuk, wuv,
                                     need_mask=need_mask)
        y, conv_rows = _out_ffn(x2d, ga, gb, attn.reshape(b * s, -1), prev_conv, wb, wo, norm_ffn_g, wup,
                                conv_w, conv_b, wdn, final_norm_g[None, :], seg_rows=s, tm=FFN_ROWS)
        outs = (y.reshape(b, s, D_MODEL), ckv.reshape(1, b, s, KV_LORA), kro.reshape(1, b, s, ROPE_DIM),
                conv_rows[None])
        if not is_prompt:
            outs = outs + (v_rows.reshape(1, b, s, GMLP_WIDTH),)
        return outs

    pos_p = jnp.arange(sp, dtype=jnp.int32)
    pos_s = jnp.tile(past + jnp.arange(ss, dtype=jnp.int32), (max(PREP_ROWS // ss, 1),))
    conv_zero = jnp.zeros((bp, CONV_W - 1, 2 * D_FF), x_prompt.dtype)

    y_p, ckv_p, kr_p, cv_p = run_stream(x_prompt, pos_p, GMLP_CHUNK, conv_zero, is_prompt=True)
    y_s, ckv_s, kr_s, cv_s, gv_s = run_stream(x_sample, pos_s, ss, state_ffn_conv[0], is_prompt=False)
    return (y_p, y_s, ckv_p, kr_p, cv_p, ckv_s, kr_s, cv_s, gv_s)
```

```python
import functools

import numpy as np
import jax
import jax.numpy as jnp
from jax import lax
from jax.experimental import pallas as pl
from jax.experimental.pallas import tpu as pltpu

D_MODEL = 1024
CHUNK = 64
GMLP_CHUNK = 128
GMLP_GROUPS = 8
GMLP_WIDTH = 1024
GMLP_GROUP_DIM = GMLP_WIDTH // GMLP_GROUPS
MLA_HEADS = 8
Q_LORA = 512
KV_LORA = 512
NOPE_DIM = 128
ROPE_DIM = 64
V_DIM = 128
QK_DIM = NOPE_DIM + ROPE_DIM
ROPE_BASE = 10000.0
ATTN_SCALE = QK_DIM ** -0.5
NEG_INF = -1e30
LOG2_E = 1.4426950408889634
D_FF = 2816
CONV_W = 3
EPS = 1e-6
OFF_Q = 2 * GMLP_WIDTH
OFF_KV = OFF_Q + Q_LORA
OFF_GATE = OFF_KV + KV_LORA + ROPE_DIM

LANES = 128
HALF_ROPE = ROPE_DIM // 2
FF_CHUNK = 256
N_FF_CHUNKS = D_FF // FF_CHUNK
VMEM_LIMIT = 56 * 1024 * 1024
PREP_ROWS = 256
FLASH_ROWS = 512
FFN_ROWS = 512

BF16 = jnp.bfloat16
F32 = jnp.float32


def _rms(x, g):
    return x * lax.rsqrt(jnp.mean(x * x, axis=-1, keepdims=True) + EPS) * g


def _dot(a, b):
    return jnp.dot(a, b, preferred_element_type=F32)


def _dot_nt(a, b):
    return lax.dot_general(a, b, (((1,), (1,)), ((), ())), preferred_element_type=F32)


def _resident(shape):
    nd = len(shape)
    return pl.BlockSpec(shape, lambda *_: (0,) * nd, pipeline_mode=pl.Buffered(1))


def _gmlp_gate_kernel(x_ref, g_ref, wuv_ref, wgate_ref, lng_ref, lnb_ref, ws_ref, bs_ref, wa_ref,
                      ga_ref, gb_ref, *maybe_v_ref, seg):
    tm = x_ref.shape[0]
    xn = _rms(x_ref[...], g_ref[...]).astype(BF16)
    ge = jax.nn.gelu(_dot(xn, wuv_ref[...]))
    u = ge[:, :GMLP_WIDTH]
    v = ge[:, GMLP_WIDTH:]
    vc = v - jnp.mean(v, axis=-1, keepdims=True)
    var = jnp.mean(vc * vc, axis=-1, keepdims=True)
    v = vc * lax.rsqrt(var + EPS) * lng_ref[...] + lnb_ref[...]
    if maybe_v_ref:
        maybe_v_ref[0][...] = v
    vb = v.astype(BF16)

    row = lax.broadcasted_iota(jnp.int32, (GMLP_CHUNK, GMLP_CHUNK), 0)
    col = lax.broadcasted_iota(jnp.int32, (GMLP_CHUNK, GMLP_CHUNK), 1)
    keep = (col <= row) & ((col // seg) == (row // seg))
    wm = [jnp.where(keep, ws_ref[g], 0.0).astype(BF16) for g in range(GMLP_GROUPS)]
    bias = bs_ref[...]
    blocks = []
    for c in range(tm // GMLP_CHUNK):
        rows = slice(c * GMLP_CHUNK, (c + 1) * GMLP_CHUNK)
        mix = [_dot(wm[g], vb[rows, g * GMLP_GROUP_DIM:(g + 1) * GMLP_GROUP_DIM])
               for g in range(GMLP_GROUPS)]
        blocks.append(u[rows] * (jnp.concatenate(mix, axis=1) + bias))
    a_out = jnp.concatenate(blocks, axis=0).astype(BF16)
    pa = _dot(a_out, wa_ref[...])
    gates = jax.nn.sigmoid(_dot(xn, wgate_ref[...]))
    ga_ref[...] = (gates[:, :D_MODEL] * pa).astype(BF16)
    gb_ref[...] = gates[:, D_MODEL:].astype(BF16)


def _gmlp_gate(x2d, norm_g, wuv, wgate, ln_g, ln_b, ws, bias_full, wa, *, seg, want_v, tm):
    n = x2d.shape[0]
    row_spec = pl.BlockSpec((tm, D_MODEL), lambda i: (i, 0))
    out_shape = [jax.ShapeDtypeStruct((n, D_MODEL), BF16), jax.ShapeDtypeStruct((n, D_MODEL), BF16)]
    out_specs = [row_spec, row_spec]
    if want_v:
        out_shape.append(jax.ShapeDtypeStruct((n, GMLP_WIDTH), F32))
        out_specs.append(row_spec)
    return pl.pallas_call(
        functools.partial(_gmlp_gate_kernel, seg=seg),
        grid=(n // tm,),
        in_specs=[row_spec, _resident(norm_g.shape), _resident(wuv.shape), _resident(wgate.shape),
                  _resident(ln_g.shape), _resident(ln_b.shape), _resident(ws.shape),
                  _resident(bias_full.shape), _resident(wa.shape)],
        out_specs=out_specs,
        out_shape=out_shape,
        compiler_params=pltpu.CompilerParams(dimension_semantics=("arbitrary",),
                                             vmem_limit_bytes=VMEM_LIMIT),
        name="gmlp_gate",
    )(x2d, norm_g, wuv, wgate, ln_g, ln_b, ws, bias_full, wa)


def _mla_prep_kernel(x_ref, g_ref, wmla_ref, qg_ref, wuq_ref, kvg_ref, cos_ref, sin_ref, *rest, expand):
    if expand:
        wuk_ref, wuv_ref, qn_ref, qr_ref, ckv_ref, kro_ref, kn_ref, kr2_ref, vv_ref = rest
    else:
        qn_ref, qr_ref, ckv_ref, kro_ref, kdup_ref = rest
    xn = _rms(x_ref[...], g_ref[...]).astype(BF16)
    z = _dot(xn, wmla_ref[...])
    q_lat = z[:, :Q_LORA]
    ckv_raw = z[:, Q_LORA:Q_LORA + KV_LORA]
    k_main = z[:, Q_LORA + KV_LORA:Q_LORA + KV_LORA + LANES]
    k_rot = z[:, Q_LORA + KV_LORA + LANES:]
    cos = cos_ref[...]
    sin = sin_ref[...]

    q = _dot(_rms(q_lat, qg_ref[...]).astype(BF16), wuq_ref[...])
    n_nope = MLA_HEADS * NOPE_DIM
    n_rope = MLA_HEADS * ROPE_DIM
    reps = n_rope // LANES
    cos_q = jnp.concatenate([cos] * reps, axis=1)
    sin_q = jnp.concatenate([sin] * reps, axis=1)
    q_scale = ATTN_SCALE * LOG2_E if expand else ATTN_SCALE
    qn_ref[...] = (q[:, :n_nope] * q_scale).astype(BF16)
    q_rope = q[:, n_nope:n_nope + n_rope] * cos_q + q[:, n_nope + n_rope:] * sin_q
    qr_ref[...] = (q_rope * q_scale).astype(BF16)

    c_kv = _rms(ckv_raw, kvg_ref[...])
    ckv_ref[...] = c_kv
    k_dup = k_main * cos + k_rot * sin
    kro_ref[...] = k_dup[:, :ROPE_DIM]
    if expand:
        cb = c_kv.astype(BF16)
        kn_ref[...] = _dot(cb, wuk_ref[...]).astype(BF16)
        for h in range(MLA_HEADS):
            vv_ref[0, h, 0] = _dot_nt(wuv_ref[h * V_DIM:(h + 1) * V_DIM, :], cb).astype(BF16)
        lane = lax.broadcasted_iota(jnp.int32, k_dup.shape, 1)
        kr2_ref[:, :LANES] = jnp.where(lane < ROPE_DIM, k_dup, 0.0).astype(BF16)
        kr2_ref[:, LANES:] = jnp.where(lane >= ROPE_DIM, k_dup, 0.0).astype(BF16)
    else:
        kdup_ref[...] = k_dup.astype(BF16)


def _mla_prep(x2d, norm_g, wmla, q_g, wuq, kv_g, cos_t, sin_t, wuk, wuv, *, expand, tm, seq, kv_tile):
    n = x2d.shape[0]
    table_blocks = cos_t.shape[0] // tm

    def rows(width):
        return pl.BlockSpec((tm, width), lambda i: (i, 0))

    table_spec = pl.BlockSpec((tm, LANES), lambda i: (i % table_blocks, 0))

    n_nope = MLA_HEADS * NOPE_DIM
    n_rope = MLA_HEADS * ROPE_DIM
    in_arrays = [x2d, norm_g, wmla, q_g, wuq, kv_g, cos_t, sin_t]
    in_specs = [rows(D_MODEL), _resident(norm_g.shape), _resident(wmla.shape), _resident(q_g.shape),
                _resident(wuq.shape), _resident(kv_g.shape), table_spec, table_spec]
    out_shape = [jax.ShapeDtypeStruct((n, n_nope), BF16), jax.ShapeDtypeStruct((n, n_rope), BF16),
                 jax.ShapeDtypeStruct((n, KV_LORA), F32), jax.ShapeDtypeStruct((n, ROPE_DIM), F32)]
    out_specs = [rows(n_nope), rows(n_rope), rows(KV_LORA), rows(ROPE_DIM)]
    if expand:
        in_arrays += [wuk, wuv]
        in_specs += [_resident(wuk.shape), _resident(wuv.shape)]
        tiles_per_seq = seq // tm
        sub = kv_tile // tm
        vt_spec = pl.BlockSpec(
            (1, MLA_HEADS, 1, V_DIM, tm),
            lambda i: (i // tiles_per_seq, 0, (i % tiles_per_seq) // sub, 0, (i % tiles_per_seq) % sub))
        out_shape += [jax.ShapeDtypeStruct((n, n_nope), BF16), jax.ShapeDtypeStruct((n, 2 * LANES), BF16),
                      jax.ShapeDtypeStruct((n // seq, MLA_HEADS, seq // kv_tile, V_DIM, kv_tile), BF16)]
        out_specs += [rows(n_nope), rows(2 * LANES), vt_spec]
    else:
        out_shape += [jax.ShapeDtypeStruct((n, LANES), BF16)]
        out_specs += [rows(LANES)]
    return pl.pallas_call(
        functools.partial(_mla_prep_kernel, expand=expand),
        grid=(n // tm,),
        in_specs=in_specs,
        out_specs=out_specs,
        out_shape=out_shape,
        compiler_params=pltpu.CompilerParams(dimension_semantics=("arbitrary",),
                                             vmem_limit_bytes=VMEM_LIMIT),
        name="mla_prep",
    )(*in_arrays)


def _flash_kernel(qn_ref, qr_ref, kn_ref, kr_ref, vt_ref, o_ref, s_ref, p_ref, a_ref, m_ref, l_ref, acc_ref, *, t):
    n = pl.program_id(2) + 1
    m_ref[...] = jnp.full(m_ref.shape, -jnp.inf, F32)
    l_ref[...] = jnp.zeros(l_ref.shape, F32)
    acc_ref[...] = jnp.zeros(acc_ref.shape, F32)

    def scores(j, slot):
        start = pl.multiple_of(j * t, t)
        k = jnp.concatenate([kn_ref[0, pl.ds(start, t), :], kr_ref[0, pl.ds(start, t), :]], axis=1)
        q = jnp.concatenate([qn_ref[0], qr_ref[0]], axis=1)
        s_ref[slot] = _dot_nt(k, q)

    def softmax(slot, masked):
        st = s_ref[slot]
        if masked:
            key = lax.broadcasted_iota(jnp.int32, st.shape, 0)
            qry = lax.broadcasted_iota(jnp.int32, st.shape, 1)
            st = jnp.where((key // CHUNK) <= (qry // CHUNK), st, NEG_INF)
        m_old = m_ref[...]
        m_new = jnp.maximum(m_old, jnp.max(st, axis=0, keepdims=True))
        alpha = jnp.exp2(m_old - m_new)
        p = jnp.exp2(st - m_new)
        l_ref[...] = alpha * l_ref[...] + jnp.sum(p, axis=0, keepdims=True)
        p_ref[slot] = p.astype(BF16)
        a_ref[slot] = alpha
        m_ref[...] = m_new

    def values(j, slot):
        acc_ref[...] = a_ref[slot] * acc_ref[...] + _dot(vt_ref[0, 0, j], p_ref[slot])

    scores(0, 0)

    @pl.when(n == 1)
    def _():
        softmax(0, True)
        values(0, 0)

    @pl.when(n > 1)
    def _():
        scores(1, 1)
        softmax(0, False)

    def pair(c, carry):
        i = 2 * c + 1
        scores(i + 1, 0)
        softmax(1, False)
        values(i - 1, 0)
        scores(i + 2, 1)
        softmax(0, False)
        values(i, 1)
        return carry

    lax.fori_loop(0, jnp.maximum(n - 2, 0) // 2, pair, 0)

    @pl.when((n > 1) & (n % 2 == 0))
    def _():
        softmax(1, True)
        values(n - 2, 0)
        values(n - 1, 1)

    @pl.when((n > 1) & (n % 2 == 1))
    def _():
        scores(n - 1, 0)
        softmax(1, False)
        values(n - 3, 0)
        softmax(0, True)
        values(n - 2, 1)
        values(n - 1, 0)

    o_ref[0] = (acc_ref[...] / l_ref[...]).T.astype(o_ref.dtype)


def _flash_attention(qn, qr, kn, kr2, vt, *, t):
    b, s, _ = qn.shape
    n_kb = s // t
    return pl.pallas_call(
        functools.partial(_flash_kernel, t=t),
        grid=(b, MLA_HEADS, s // t),
        in_specs=[
            pl.BlockSpec((1, t, LANES), lambda bi, h, qi: (bi, qi, h)),
            pl.BlockSpec((1, t, LANES), lambda bi, h, qi: (bi, qi, h // 2)),
            pl.BlockSpec((1, s, LANES), lambda bi, h, qi: (bi, 0, h)),
            pl.BlockSpec((1, s, LANES), lambda bi, h, qi: (bi, 0, h % 2)),
            pl.BlockSpec((1, 1, n_kb, V_DIM, t), lambda bi, h, qi: (bi, h, 0, 0, 0)),
        ],
        out_specs=pl.BlockSpec((1, t, LANES), lambda bi, h, qi: (bi, qi, h)),
        out_shape=jax.ShapeDtypeStruct((b, s, MLA_HEADS * V_DIM), BF16),
        scratch_shapes=[pltpu.VMEM((2, t, t), F32), pltpu.VMEM((2, t, t), BF16), pltpu.VMEM((2, 1, t), F32),
                        pltpu.VMEM((1, t), F32), pltpu.VMEM((1, t), F32), pltpu.VMEM((V_DIM, t), F32)],
        compiler_params=pltpu.CompilerParams(dimension_semantics=("arbitrary", "arbitrary", "arbitrary"),
                                             vmem_limit_bytes=VMEM_LIMIT),
        name="flash_attention",
    )(qn, qr, kn, kr2, vt)


def _sample_attn_kernel(qn_ref, qr_ref, ckvn_ref, kdup_ref, cache_ref, krc_ref, wuk_ref, wuv_ref, o_ref,
                        *, past, n_new, need_mask):
    qn = qn_ref[0]
    qr = qr_ref[0]
    lane = lax.broadcasted_iota(jnp.int32, (n_new, LANES), 1)
    q_lat, q_rope = [], []
    for h in range(MLA_HEADS):
        q_lat.append(_dot_nt(qn[:, h * NOPE_DIM:(h + 1) * NOPE_DIM],
                             wuk_ref[:, h * NOPE_DIM:(h + 1) * NOPE_DIM]).astype(BF16))
        pair = qr[:, (h // 2) * LANES:(h // 2 + 1) * LANES]
        half = (lane < ROPE_DIM) if h % 2 == 0 else (lane >= ROPE_DIM)
        q_rope.append(jnp.where(half, pair, jnp.zeros_like(pair)))
    q_all = jnp.concatenate([jnp.concatenate(q_lat, axis=0), jnp.concatenate(q_rope, axis=0)], axis=1)

    k_cache = jnp.concatenate([cache_ref[0, 0].astype(BF16), krc_ref[0]], axis=1)
    pad = LANES - n_new
    c_new = jnp.concatenate([ckvn_ref[0].astype(BF16), jnp.zeros((pad, KV_LORA), BF16)], axis=0)
    r_new = jnp.concatenate([kdup_ref[0], jnp.zeros((pad, LANES), BF16)], axis=0)
    k_new = jnp.concatenate([c_new, r_new], axis=1)

    rows_total = MLA_HEADS * n_new
    half_rows = rows_total // 2
    heads_per_half = MLA_HEADS // 2
    outs = []
    for part in range(2):
        qp = q_all[part * half_rows:(part + 1) * half_rows]
        s_c = _dot_nt(qp, k_cache)
        s_n = _dot_nt(qp, k_new)
        col_n = lax.broadcasted_iota(jnp.int32, s_n.shape, 1)
        valid_n = col_n < n_new
        if need_mask:
            q_pos = past + lax.broadcasted_iota(jnp.int32, s_c.shape, 0) % n_new
            k_pos = lax.broadcasted_iota(jnp.int32, s_c.shape, 1)
            s_c = jnp.where((k_pos // CHUNK) <= (q_pos // CHUNK), s_c, NEG_INF)
            q_pos_n = past + lax.broadcasted_iota(jnp.int32, s_n.shape, 0) % n_new
            valid_n = valid_n & (((past + col_n) // CHUNK) <= (q_pos_n // CHUNK))
        m = jnp.maximum(jnp.max(s_c, axis=1, keepdims=True),
                        jnp.max(jnp.where(valid_n, s_n, NEG_INF), axis=1, keepdims=True))
        p_c = jnp.exp(s_c - m)
        p_n = jnp.where(col_n < n_new, jnp.exp(jnp.where(valid_n, s_n, NEG_INF) - m), 0.0)
        denom = jnp.sum(p_c, axis=1, keepdims=True) + jnp.sum(p_n, axis=1, keepdims=True)
        o_lat = _dot(p_c.astype(BF16), k_cache[:, :KV_LORA]) + _dot(p_n.astype(BF16), c_new)
        o_lat = (o_lat / denom).astype(BF16)
        for hh in range(heads_per_half):
            h = part * heads_per_half + hh
            outs.append(_dot(o_lat[hh * n_new:(hh + 1) * n_new],
                             wuv_ref[:, h * V_DIM:(h + 1) * V_DIM]))
    o_ref[0] = jnp.concatenate(outs, axis=1).astype(o_ref.dtype)


def _sample_attention(qn, qr, ckv_new, kdup, cache_ckv, krc, wuk, wuv, *, need_mask):
    nb, n_new, _ = qn.shape
    past = cache_ckv.shape[2]
    return pl.pallas_call(
        functools.partial(_sample_attn_kernel, past=past, n_new=n_new, need_mask=need_mask),
        grid=(nb,),
        in_specs=[
            pl.BlockSpec((1, n_new, qn.shape[2]), lambda i: (i, 0, 0)),
            pl.BlockSpec((1, n_new, qr.shape[2]), lambda i: (i, 0, 0)),
            pl.BlockSpec((1, n_new, KV_LORA), lambda i: (i, 0, 0)),
            pl.BlockSpec((1, n_new, LANES), lambda i: (i, 0, 0)),
            pl.BlockSpec((1, 1, past, KV_LORA), lambda i: (0, i, 0, 0)),
            pl.BlockSpec((1, past, LANES), lambda i: (i, 0, 0)),
            _resident(wuk.shape), _resident(wuv.shape),
        ],
        out_specs=pl.BlockSpec((1, n_new, MLA_HEADS * V_DIM), lambda i: (i, 0, 0)),
        out_shape=jax.ShapeDtypeStruct((nb, n_new, MLA_HEADS * V_DIM), BF16),
        compiler_params=pltpu.CompilerParams(dimension_semantics=("arbitrary",),
                                             vmem_limit_bytes=VMEM_LIMIT),
        name="sample_attention",
    )(qn, qr, ckv_new, kdup, cache_ckv, krc, wuk, wuv)


def _out_ffn_kernel(x_ref, ga_ref, gb_ref, attn_ref, prev_ref, wb_ref, wo_ref, fg_ref, wup_ref, cw_ref, cb_ref,
                    wdn_ref, fing_ref, y_ref, conv_ref, carry_ref, *, seg_rows, tiles_per_seq):
    tm = x_ref.shape[0]
    nseg = tm // seg_rows
    merged = ga_ref[...].astype(F32) + gb_ref[...].astype(F32) * _dot(attn_ref[...], wb_ref[...])
    h = x_ref[...] + _dot(merged.astype(BF16), wo_ref[...])
    hn = _rms(h, fg_ref[...]).astype(BF16)

    if tiles_per_seq == 1:
        prev = prev_ref[...]
    else:
        first = (pl.program_id(0) % tiles_per_seq) == 0
        prev = jnp.where(first, prev_ref[...], carry_ref[...][None])

    row = lax.broadcasted_iota(jnp.int32, (seg_rows, FF_CHUNK), 0)

    def conv(up, c0):
        cols = slice(c0, c0 + FF_CHUNK)
        w = cw_ref[:, cols]
        outs = []
        for s in range(nseg):
            u = up[s * seg_rows:(s + 1) * seg_rows]
            p0 = prev[s, 0:1, cols]
            p1 = prev[s, 1:2, cols]
            d1 = jnp.where(row == 0, p1, pltpu.roll(u, 1, 0))
            d2 = jnp.where(row == 0, p0, jnp.where(row == 1, p1, pltpu.roll(u, 2, 0)))
            outs.append(cb_ref[:, cols] + d2 * w[0:1] + d1 * w[1:2] + u * w[2:3])
            tail = u[seg_rows - (CONV_W - 1):]
            conv_ref[s, :, cols] = tail
            if tiles_per_seq > 1:
                carry_ref[:, cols] = tail
        return outs[0] if nseg == 1 else jnp.concatenate(outs, axis=0)

    acc = jnp.zeros((tm, D_MODEL), F32)
    for j in range(N_FF_CHUNKS):
        c_val = j * FF_CHUNK
        c_gate = D_FF + j * FF_CHUNK
        val = conv(_dot(hn, wup_ref[:, c_val:c_val + FF_CHUNK]), c_val)
        gate = conv(_dot(hn, wup_ref[:, c_gate:c_gate + FF_CHUNK]), c_gate)
        act = (jax.nn.silu(gate) * val).astype(BF16)
        acc = acc + _dot(act, wdn_ref[c_val:c_val + FF_CHUNK, :])
    y_ref[...] = _rms(h + acc, fing_ref[...])


def _out_ffn(x2d, ga, gb, attn, prev, wb, wo, ffn_g, wup, conv_w, conv_b, wdn, fin_g, *, seg_rows, tm):
    n = x2d.shape[0]
    nseg = tm // seg_rows if seg_rows < tm else 1
    seg = min(seg_rows, tm)
    tiles_per_seq = max(seg_rows // tm, 1)
    row_spec = pl.BlockSpec((tm, D_MODEL), lambda i: (i, 0))
    state_spec = pl.BlockSpec((nseg, CONV_W - 1, 2 * D_FF),
                              (lambda i: (i // tiles_per_seq, 0, 0)) if tiles_per_seq > 1 else (lambda i: (i, 0, 0)))
    return pl.pallas_call(
        functools.partial(_out_ffn_kernel, seg_rows=seg, tiles_per_seq=tiles_per_seq),
        grid=(n // tm,),
        in_specs=[row_spec, row_spec, row_spec, row_spec, state_spec,
                  _resident(wb.shape), _resident(wo.shape), _resident(ffn_g.shape), _resident(wup.shape),
                  _resident(conv_w.shape), _resident(conv_b.shape), _resident(wdn.shape),
                  _resident(fin_g.shape)],
        out_specs=[row_spec, state_spec],
        out_shape=[jax.ShapeDtypeStruct((n, D_MODEL), F32),
                   jax.ShapeDtypeStruct(prev.shape, F32)],
        scratch_shapes=[pltpu.VMEM((CONV_W - 1, 2 * D_FF), F32)],
        compiler_params=pltpu.CompilerParams(dimension_semantics=("arbitrary",),
                                             vmem_limit_bytes=VMEM_LIMIT),
        name="out_ffn",
    )(x2d, ga, gb, attn, prev, wb, wo, ffn_g, wup, conv_w, conv_b, wdn, fin_g)


def _rope_tables(pos):
    inv_freq = ROPE_BASE ** (-jnp.arange(HALF_ROPE, dtype=F32) / HALF_ROPE)
    ang = pos.astype(F32)[:, None] * inv_freq[None, :]
    reps = LANES // HALF_ROPE
    return jnp.tile(jnp.cos(ang), (1, reps)), jnp.tile(jnp.sin(ang), (1, reps))


def _rotated_cols(w):
    return jnp.concatenate([-w[..., HALF_ROPE:], w[..., :HALF_ROPE]], axis=-1)


def kernel(x_prompt, x_sample, cache_mla_ckv, cache_mla_krope, state_ffn_conv, norm_mix_g, w_in, gmlp_ln_g,
           gmlp_ln_b, gmlp_w_s, gmlp_b_s, mla_q_norm_g, mla_w_uq, mla_kv_norm_g, mla_w_uk, mla_w_uv,
           w_proj_a, w_proj_b, w_out, norm_ffn_g, ffn_w_up, ffn_conv_w, ffn_conv_b, ffn_w_down,
           final_norm_g):
    assert w_in.shape[0] == 1, "single-layer step"
    bp, sp, _ = x_prompt.shape
    bs, ss, _ = x_sample.shape
    past = cache_mla_ckv.shape[2]
    assert sp % GMLP_CHUNK == 0 and ss <= GMLP_CHUNK and GMLP_CHUNK % ss == 0

    w0 = w_in[0]
    wuv_in = w0[:, :OFF_Q].astype(BF16)
    wgate = w0[:, OFF_GATE:].astype(BF16)
    w_kr = w0[:, OFF_KV + KV_LORA:OFF_GATE]
    w_kr_rot = _rotated_cols(w_kr)
    wmla = jnp.concatenate([w0[:, OFF_Q:OFF_KV + KV_LORA], w_kr, w_kr, w_kr_rot, w_kr_rot], axis=1).astype(BF16)
    uq = mla_w_uq[0]
    uq_rope = uq[:, :, NOPE_DIM:]
    wuq = jnp.concatenate([uq[:, :, :NOPE_DIM].reshape(Q_LORA, -1), uq_rope.reshape(Q_LORA, -1),
                           _rotated_cols(uq_rope).reshape(Q_LORA, -1)], axis=1).astype(BF16)
    wuk = mla_w_uk[0].reshape(KV_LORA, -1).astype(BF16)
    wuv = mla_w_uv[0].reshape(KV_LORA, -1).astype(BF16)
    wa = w_proj_a[0].astype(BF16)
    wb = w_proj_b[0].astype(BF16)
    wo = w_out[0].astype(BF16)
    wup = ffn_w_up[0].astype(BF16)
    wdn = ffn_w_down[0].astype(BF16)
    conv_w = ffn_conv_w[0]
    conv_b = ffn_conv_b

    def spatial_params(n):
        reps = GMLP_CHUNK // n
        ws = jnp.tile(gmlp_w_s[0][:, :n, :n], (1, reps, reps))
        bias = jnp.tile(jnp.repeat(gmlp_b_s[0][:, :n].T, GMLP_GROUP_DIM, axis=1), (reps, 1))
        return ws, bias

    def run_stream(x, pos_row, n_spatial, prev_conv, *, is_prompt):
        b, s, _ = x.shape
        x2d = x.reshape(b * s, D_MODEL)
        ws, bias = spatial_params(n_spatial)
        res = _gmlp_gate(x2d, norm_mix_g, wuv_in, wgate, gmlp_ln_g, gmlp_ln_b, ws, bias, wa,
                         seg=n_spatial, want_v=not is_prompt, tm=PREP_ROWS)
        cos_t, sin_t = _rope_tables(pos_row)
        prep = _mla_prep(x2d, norm_mix_g, wmla, mla_q_norm_g, wuq, mla_kv_norm_g, cos_t, sin_t, wuk,
                         wuv.T if is_prompt else wuv, expand=is_prompt, tm=PREP_ROWS, seq=s, kv_tile=FLASH_ROWS)
        if is_prompt:
            ga, gb = res
            qn, qr, ckv, kro, kn, kr2, vt = prep
            r3 = lambda a: a.reshape(b, s, a.shape[-1])
            attn = _flash_attention(r3(qn), r3(qr), r3(kn), r3(kr2), vt, t=FLASH_ROWS)
        else:
            ga, gb, v_rows = res
            qn, qr, ckv, kro, kdup = prep
            r3 = lambda a: a.reshape(b, s, a.shape[-1])
            krc = jnp.concatenate([cache_mla_krope[0], cache_mla_krope[0]], axis=-1).astype(BF16)
            k_pos = np.arange(past + s)
            q_pos = past + np.arange(s)
            need_mask = not bool(np.all((k_pos[None, :] // CHUNK) <= (q_pos[:, None] // CHUNK)))
            attn = _sample_attention(r3(qn), r3(qr), r3(ckv), r3(kdup), cache_mla_ckv, krc, wuk, wuv,
                                     need_mask=need_mask)
        y, conv_rows = _out_ffn(x2d, ga, gb, attn.reshape(b * s, -1), prev_conv, wb, wo, norm_ffn_g, wup,
                                conv_w, conv_b, wdn, final_norm_g[None, :], seg_rows=s, tm=FFN_ROWS)
        outs = (y.reshape(b, s, D_MODEL), ckv.reshape(1, b, s, KV_LORA), kro.reshape(1, b, s, ROPE_DIM),
                conv_rows[None])
        if not is_prompt:
            outs = outs + (v_rows.reshape(1, b, s, GMLP_WIDTH),)
        return outs

    pos_p = jnp.arange(sp, dtype=jnp.int32)
    pos_s = jnp.tile(past + jnp.arange(ss, dtype=jnp.int32), (max(PREP_ROWS // ss, 1),))
    conv_zero = jnp.zeros((bp, CONV_W - 1, 2 * D_FF), x_prompt.dtype)

    y_p, ckv_p, kr_p, cv_p = run_stream(x_prompt, pos_p, GMLP_CHUNK, conv_zero, is_prompt=True)
    y_s, ckv_s, kr_s, cv_s, gv_s = run_stream(x_sample, pos_s, ss, state_ffn_conv[0], is_prompt=False)
    return (y_p, y_s, ckv_p, kr_p, cv_p, ckv_s, kr_s, cv_s, gv_s)
```

```python
import functools

import numpy as np
import jax
import jax.numpy as jnp
from jax import lax
from jax.experimental import pallas as pl
from jax.experimental.pallas import tpu as pltpu

D_MODEL = 1024
CHUNK = 64
GMLP_CHUNK = 128
GMLP_GROUPS = 8
GMLP_WIDTH = 1024
GMLP_GROUP_DIM = GMLP_WIDTH // GMLP_GROUPS
MLA_HEADS = 8
Q_LORA = 512
KV_LORA = 512
NOPE_DIM = 128
ROPE_DIM = 64
V_DIM = 128
QK_DIM = NOPE_DIM + ROPE_DIM
ROPE_BASE = 10000.0
ATTN_SCALE = QK_DIM ** -0.5
NEG_INF = -1e30
LOG2_E = 1.4426950408889634
D_FF = 2816
CONV_W = 3
EPS = 1e-6
OFF_Q = 2 * GMLP_WIDTH
OFF_KV = OFF_Q + Q_LORA
OFF_GATE = OFF_KV + KV_LORA + ROPE_DIM

LANES = 128
SUBLANES = 8
HALF_ROPE = ROPE_DIM // 2
V_PAD = 16
FF_CHUNK = 256
N_FF_CHUNKS = D_FF // FF_CHUNK
VMEM_LIMIT = 56 * 1024 * 1024
PREP_ROWS = 256
FLASH_ROWS = 1024
FFN_ROWS = 512

BF16 = jnp.bfloat16
F32 = jnp.float32


def _rms(x, g):
    return x * lax.rsqrt(jnp.mean(x * x, axis=-1, keepdims=True) + EPS) * g


def _dot(a, b):
    return jnp.dot(a, b, preferred_element_type=F32)


def _dot_nt(a, b):
    return lax.dot_general(a, b, (((1,), (1,)), ((), ())), preferred_element_type=F32)


def _resident(shape):
    nd = len(shape)
    return pl.BlockSpec(shape, lambda *_: (0,) * nd, pipeline_mode=pl.Buffered(1))


def _gmlp_gate_kernel(x_ref, g_ref, wuv_ref, wgate_ref, lng_ref, lnb_ref, ws_ref, bs_ref, wa_ref,
                      ga_ref, gb_ref, *maybe_v_ref, seg):
    tm = x_ref.shape[0]
    xn = _rms(x_ref[...], g_ref[...]).astype(BF16)
    ge = jax.nn.gelu(_dot(xn, wuv_ref[...]))
    u = ge[:, :GMLP_WIDTH]
    v = ge[:, GMLP_WIDTH:]
    vc = v - jnp.mean(v, axis=-1, keepdims=True)
    var = jnp.mean(vc * vc, axis=-1, keepdims=True)
    v = vc * lax.rsqrt(var + EPS) * lng_ref[...] + lnb_ref[...]
    if maybe_v_ref:
        maybe_v_ref[0][...] = v
    vb = v.astype(BF16)

    row = lax.broadcasted_iota(jnp.int32, (GMLP_CHUNK, GMLP_CHUNK), 0)
    col = lax.broadcasted_iota(jnp.int32, (GMLP_CHUNK, GMLP_CHUNK), 1)
    keep = (col <= row) & ((col // seg) == (row // seg))
    wm = [jnp.where(keep, ws_ref[g], 0.0).astype(BF16) for g in range(GMLP_GROUPS)]
    bias = bs_ref[...]
    blocks = []
    for c in range(tm // GMLP_CHUNK):
        rows = slice(c * GMLP_CHUNK, (c + 1) * GMLP_CHUNK)
        mix = [_dot(wm[g], vb[rows, g * GMLP_GROUP_DIM:(g + 1) * GMLP_GROUP_DIM])
               for g in range(GMLP_GROUPS)]
        blocks.append(u[rows] * (jnp.concatenate(mix, axis=1) + bias))
    a_out = jnp.concatenate(blocks, axis=0).astype(BF16)
    pa = _dot(a_out, wa_ref[...])
    gates = jax.nn.sigmoid(_dot(xn, wgate_ref[...]))
    ga_ref[...] = (gates[:, :D_MODEL] * pa).astype(BF16)
    gb_ref[...] = gates[:, D_MODEL:].astype(BF16)


def _gmlp_gate(x2d, norm_g, wuv, wgate, ln_g, ln_b, ws, bias_full, wa, *, seg, want_v, tm):
    n = x2d.shape[0]
    row_spec = pl.BlockSpec((tm, D_MODEL), lambda i: (i, 0))
    out_shape = [jax.ShapeDtypeStruct((n, D_MODEL), BF16), jax.ShapeDtypeStruct((n, D_MODEL), BF16)]
    out_specs = [row_spec, row_spec]
    if want_v:
        out_shape.append(jax.ShapeDtypeStruct((n, GMLP_WIDTH), F32))
        out_specs.append(row_spec)
    return pl.pallas_call(
        functools.partial(_gmlp_gate_kernel, seg=seg),
        grid=(n // tm,),
        in_specs=[row_spec, _resident(norm_g.shape), _resident(wuv.shape), _resident(wgate.shape),
                  _resident(ln_g.shape), _resident(ln_b.shape), _resident(ws.shape),
                  _resident(bias_full.shape), _resident(wa.shape)],
        out_specs=out_specs,
        out_shape=out_shape,
        compiler_params=pltpu.CompilerParams(dimension_semantics=("arbitrary",),
                                             vmem_limit_bytes=VMEM_LIMIT),
        name="gmlp_gate",
    )(x2d, norm_g, wuv, wgate, ln_g, ln_b, ws, bias_full, wa)


def _mla_prep_kernel(x_ref, g_ref, wmla_ref, qg_ref, wuq_ref, kvg_ref, cos_ref, sin_ref, *rest, expand):
    if expand:
        wuk_ref, wuv_ref, qn_ref, qr_ref, ckv_ref, kro_ref, kn_ref, kr2_ref, vv_ref = rest
    else:
        qn_ref, qr_ref, ckv_ref, kro_ref, kdup_ref = rest
    xn = _rms(x_ref[...], g_ref[...]).astype(BF16)
    z = _dot(xn, wmla_ref[...])
    q_lat = z[:, :Q_LORA]
    ckv_raw = z[:, Q_LORA:Q_LORA + KV_LORA]
    k_main = z[:, Q_LORA + KV_LORA:Q_LORA + KV_LORA + LANES]
    k_rot = z[:, Q_LORA + KV_LORA + LANES:]
    cos = cos_ref[...]
    sin = sin_ref[...]

    q = _dot(_rms(q_lat, qg_ref[...]).astype(BF16), wuq_ref[...])
    n_nope = MLA_HEADS * NOPE_DIM
    n_rope = MLA_HEADS * ROPE_DIM
    reps = n_rope // LANES
    cos_q = jnp.concatenate([cos] * reps, axis=1)
    sin_q = jnp.concatenate([sin] * reps, axis=1)
    q_scale = ATTN_SCALE * LOG2_E if expand else ATTN_SCALE
    qn_ref[...] = (q[:, :n_nope] * q_scale).astype(BF16)
    q_rope = q[:, n_nope:n_nope + n_rope] * cos_q + q[:, n_nope + n_rope:] * sin_q
    qr_ref[...] = (q_rope * q_scale).astype(BF16)

    c_kv = _rms(ckv_raw, kvg_ref[...])
    ckv_ref[...] = c_kv
    k_dup = k_main * cos + k_rot * sin
    kro_ref[...] = k_dup[:, :ROPE_DIM]
    if expand:
        cb = c_kv.astype(BF16)
        kn_ref[...] = _dot(cb, wuk_ref[...]).astype(BF16)
        pad_row = lax.broadcasted_iota(jnp.int32, (V_PAD, cb.shape[0]), 0)
        ones_rows = jnp.where(pad_row == 0, 1.0, 0.0).astype(BF16)
        for h in range(MLA_HEADS):
            vv_ref[0, h, 0, :V_DIM, :] = _dot_nt(wuv_ref[h * V_DIM:(h + 1) * V_DIM, :], cb).astype(BF16)
            vv_ref[0, h, 0, V_DIM:, :] = ones_rows
        lane = lax.broadcasted_iota(jnp.int32, k_dup.shape, 1)
        kr2_ref[:, :LANES] = jnp.where(lane < ROPE_DIM, k_dup, 0.0).astype(BF16)
        kr2_ref[:, LANES:] = jnp.where(lane >= ROPE_DIM, k_dup, 0.0).astype(BF16)
    else:
        kdup_ref[...] = k_dup.astype(BF16)


def _mla_prep(x2d, norm_g, wmla, q_g, wuq, kv_g, cos_t, sin_t, wuk, wuv, *, expand, tm, seq, kv_tile):
    n = x2d.shape[0]
    table_blocks = cos_t.shape[0] // tm

    def rows(width):
        return pl.BlockSpec((tm, width), lambda i: (i, 0))

    table_spec = pl.BlockSpec((tm, LANES), lambda i: (i % table_blocks, 0))

    n_nope = MLA_HEADS * NOPE_DIM
    n_rope = MLA_HEADS * ROPE_DIM
    in_arrays = [x2d, norm_g, wmla, q_g, wuq, kv_g, cos_t, sin_t]
    in_specs = [rows(D_MODEL), _resident(norm_g.shape), _resident(wmla.shape), _resident(q_g.shape),
                _resident(wuq.shape), _resident(kv_g.shape), table_spec, table_spec]
    out_shape = [jax.ShapeDtypeStruct((n, n_nope), BF16), jax.ShapeDtypeStruct((n, n_rope), BF16),
                 jax.ShapeDtypeStruct((n, KV_LORA), F32), jax.ShapeDtypeStruct((n, ROPE_DIM), F32)]
    out_specs = [rows(n_nope), rows(n_rope), rows(KV_LORA), rows(ROPE_DIM)]
    if expand:
        in_arrays += [wuk, wuv]
        in_specs += [_resident(wuk.shape), _resident(wuv.shape)]
        tiles_per_seq = seq // tm
        sub = kv_tile // tm
        vt_spec = pl.BlockSpec(
            (1, MLA_HEADS, 1, V_DIM + V_PAD, tm),
            lambda i: (i // tiles_per_seq, 0, (i % tiles_per_seq) // sub, 0, (i % tiles_per_seq) % sub))
        out_shape += [jax.ShapeDtypeStruct((n, n_nope), BF16), jax.ShapeDtypeStruct((n, 2 * LANES), BF16),
                      jax.ShapeDtypeStruct((n // seq, MLA_HEADS, seq // kv_tile, V_DIM + V_PAD, kv_tile), BF16)]
        out_specs += [rows(n_nope), rows(2 * LANES), vt_spec]
    else:
        out_shape += [jax.ShapeDtypeStruct((n, LANES), BF16)]
        out_specs += [rows(LANES)]
    return pl.pallas_call(
        functools.partial(_mla_prep_kernel, expand=expand),
        grid=(n // tm,),
        in_specs=in_specs,
        out_specs=out_specs,
        out_shape=out_shape,
        compiler_params=pltpu.CompilerParams(dimension_semantics=("arbitrary",),
                                             vmem_limit_bytes=VMEM_LIMIT),
        name="mla_prep",
    )(*in_arrays)


def _flash_kernel(qn_ref, qr_ref, kn_ref, kr_ref, vt_ref, o_ref, s0_ref, s1_ref, p0_ref, p1_ref, bm0_ref, bm1_ref,
                  a0_ref, a1_ref, m_ref, acc_ref, *, t):
    s_ref, p_ref, bm_ref, a_ref = (s0_ref, s1_ref), (p0_ref, p1_ref), (bm0_ref, bm1_ref), (a0_ref, a1_ref)
    n = pl.program_id(2) + 1
    m_ref[...] = jnp.full(m_ref.shape, -jnp.inf, F32)
    acc_ref[...] = jnp.zeros(acc_ref.shape, F32)

    def scores(j, slot, masked=False):
        start = pl.multiple_of(j * t, t)
        k = jnp.concatenate([kn_ref[0, pl.ds(start, t), :], kr_ref[0, pl.ds(start, t), :]], axis=1)
        q = jnp.concatenate([qn_ref[0], qr_ref[0]], axis=1)
        st = _dot_nt(k, q)
        if masked:
            key = lax.broadcasted_iota(jnp.int32, st.shape, 0)
            qry = lax.broadcasted_iota(jnp.int32, st.shape, 1)
            st = jnp.where((key // CHUNK) <= (qry // CHUNK), st, NEG_INF)
        s_ref[slot][...] = st
        bm_ref[slot][...] = jnp.max(st, axis=0, keepdims=True)

    def softmax(slot):
        m_old = m_ref[...]
        m_new = jnp.maximum(m_old, bm_ref[slot][...])
        alpha = jnp.exp2(m_old - m_new)
        p_ref[slot][...] = jnp.exp2(s_ref[slot][...] - m_new).astype(BF16)
        a_ref[slot][...] = alpha
        m_ref[...] = m_new

    def values(j, slot):
        acc_ref[...] = a_ref[slot][...] * acc_ref[...] + _dot(vt_ref[0, 0, j], p_ref[slot][...])

    @pl.when(n <= 2)
    def _():
        @pl.when(n == 2)
        def _():
            scores(0, 0)
            softmax(0)
            values(0, 0)
        scores(n - 1, 1, masked=True)
        softmax(1)
        values(n - 1, 1)

    @pl.when(n >= 3)
    def _():
        scores(0, 0)
        scores(1, 1)
        softmax(0)

    def pair(c, carry):
        i = 2 * c + 1
        scores(i + 1, 0)
        softmax(1)
        values(i - 1, 0)
        scores(i + 2, 1)
        softmax(0)
        values(i, 1)
        return carry

    lax.fori_loop(0, jnp.maximum(n - 3, 0) // 2, pair, 0)

    @pl.when((n >= 3) & (n % 2 == 1))
    def _():
        scores(n - 1, 0, masked=True)
        softmax(1)
        values(n - 3, 0)
        softmax(0)
        values(n - 2, 1)
        values(n - 1, 0)

    @pl.when((n >= 3) & (n % 2 == 0))
    def _():
        scores(n - 2, 0)
        softmax(1)
        values(n - 4, 0)
        scores(n - 1, 1, masked=True)
        softmax(0)
        values(n - 3, 1)
        softmax(1)
        values(n - 2, 0)
        values(n - 1, 1)

    o_ref[0] = (acc_ref[:V_DIM, :] / acc_ref[V_DIM:V_DIM + 1, :]).T.astype(o_ref.dtype)


def _flash_attention(qn, qr, kn, kr2, vt, *, t):
    b, s, _ = qn.shape
    n_kb = s // t
    return pl.pallas_call(
        functools.partial(_flash_kernel, t=t),
        grid=(b, MLA_HEADS, s // t),
        in_specs=[
            pl.BlockSpec((1, t, LANES), lambda bi, h, qi: (bi, qi, h)),
            pl.BlockSpec((1, t, LANES), lambda bi, h, qi: (bi, qi, h // 2)),
            pl.BlockSpec((1, s, LANES), lambda bi, h, qi: (bi, 0, h)),
            pl.BlockSpec((1, s, LANES), lambda bi, h, qi: (bi, 0, h % 2)),
            pl.BlockSpec((1, 1, n_kb, V_DIM + V_PAD, t), lambda bi, h, qi: (bi, h, 0, 0, 0)),
        ],
        out_specs=pl.BlockSpec((1, t, LANES), lambda bi, h, qi: (bi, qi, h)),
        out_shape=jax.ShapeDtypeStruct((b, s, MLA_HEADS * V_DIM), BF16),
        scratch_shapes=[pltpu.VMEM((t, t), F32), pltpu.VMEM((t, t), F32),
                        pltpu.VMEM((t, t), BF16), pltpu.VMEM((t, t), BF16),
                        pltpu.VMEM((1, t), F32), pltpu.VMEM((1, t), F32),
                        pltpu.VMEM((1, t), F32), pltpu.VMEM((1, t), F32),
                        pltpu.VMEM((1, t), F32), pltpu.VMEM((V_DIM + V_PAD, t), F32)],
        compiler_params=pltpu.CompilerParams(dimension_semantics=("arbitrary", "arbitrary", "arbitrary"),
                                             vmem_limit_bytes=VMEM_LIMIT),
        name="flash_attention",
    )(qn, qr, kn, kr2, vt)


def _sample_attn_kernel(qn_ref, qr_ref, ckvn_ref, kdup_ref, cache_ref, krc_ref, wuk_ref, wuv_ref, o_ref,
                        *, past, n_new, need_mask):
    qn = qn_ref[0]
    qr = qr_ref[0]
    lane = lax.broadcasted_iota(jnp.int32, (n_new, LANES), 1)
    q_lat, q_rope = [], []
    for h in range(MLA_HEADS):
        q_lat.append(_dot_nt(qn[:, h * NOPE_DIM:(h + 1) * NOPE_DIM],
                             wuk_ref[:, h * NOPE_DIM:(h + 1) * NOPE_DIM]).astype(BF16))
        pair = qr[:, (h // 2) * LANES:(h // 2 + 1) * LANES]
        half = (lane < ROPE_DIM) if h % 2 == 0 else (lane >= ROPE_DIM)
        q_rope.append(jnp.where(half, pair, jnp.zeros_like(pair)))
    q_all = jnp.concatenate([jnp.concatenate(q_lat, axis=0), jnp.concatenate(q_rope, axis=0)], axis=1)

    k_cache = jnp.concatenate([cache_ref[0, 0].astype(BF16), krc_ref[0]], axis=1)
    pad = LANES - n_new
    c_new = jnp.concatenate([ckvn_ref[0].astype(BF16), jnp.zeros((pad, KV_LORA), BF16)], axis=0)
    r_new = jnp.concatenate([kdup_ref[0], jnp.zeros((pad, LANES), BF16)], axis=0)
    k_new = jnp.concatenate([c_new, r_new], axis=1)

    rows_total = MLA_HEADS * n_new
    half_rows = rows_total // 2
    heads_per_half = MLA_HEADS // 2
    outs = []
    for part in range(2):
        qp = q_all[part * half_rows:(part + 1) * half_rows]
        s_c = _dot_nt(qp, k_cache)
        s_n = _dot_nt(qp, k_new)
        col_n = lax.broadcasted_iota(jnp.int32, s_n.shape, 1)
        valid_n = col_n < n_new
        if need_mask:
            q_pos = past + lax.broadcasted_iota(jnp.int32, s_c.shape, 0) % n_new
            k_pos = lax.broadcasted_iota(jnp.int32, s_c.shape, 1)
            s_c = jnp.where((k_pos // CHUNK) <= (q_pos // CHUNK), s_c, NEG_INF)
            q_pos_n = past + lax.broadcasted_iota(jnp.int32, s_n.shape, 0) % n_new
            valid_n = valid_n & (((past + col_n) // CHUNK) <= (q_pos_n // CHUNK))
        m = jnp.maximum(jnp.max(s_c, axis=1, keepdims=True),
                        jnp.max(jnp.where(valid_n, s_n, NEG_INF), axis=1, keepdims=True))
        p_c = jnp.exp(s_c - m)
        p_n = jnp.where(col_n < n_new, jnp.exp(jnp.where(valid_n, s_n, NEG_INF) - m), 0.0)
        denom = jnp.sum(p_c, axis=1, keepdims=True) + jnp.sum(p_n, axis=1, keepdims=True)
        o_lat = _dot(p_c.astype(BF16), k_cache[:, :KV_LORA]) + _dot(p_n.astype(BF16), c_new)
        o_lat = (o_lat / denom).astype(BF16)
        for hh in range(heads_per_half):
            h = part * heads_per_half + hh
            outs.append(_dot(o_lat[hh * n_new:(hh + 1) * n_new],
                             wuv_ref[:, h * V_DIM:(h + 1) * V_DIM]))
    o_ref[0] = jnp.concatenate(outs, axis=1).astype(o_ref.dtype)


def _sample_attention(qn, qr, ckv_new, kdup, cache_ckv, krc, wuk, wuv, *, need_mask):
    nb, n_new, _ = qn.shape
    past = cache_ckv.shape[2]
    return pl.pallas_call(
        functools.partial(_sample_attn_kernel, past=past, n_new=n_new, need_mask=need_mask),
        grid=(nb,),
        in_specs=[
            pl.BlockSpec((1, n_new, qn.shape[2]), lambda i: (i, 0, 0)),
            pl.BlockSpec((1, n_new, qr.shape[2]), lambda i: (i, 0, 0)),
            pl.BlockSpec((1, n_new, KV_LORA), lambda i: (i, 0, 0)),
            pl.BlockSpec((1, n_new, LANES), lambda i: (i, 0, 0)),
            pl.BlockSpec((1, 1, past, KV_LORA), lambda i: (0, i, 0, 0)),
            pl.BlockSpec((1, past, LANES), lambda i: (i, 0, 0)),
            _resident(wuk.shape), _resident(wuv.shape),
        ],
        out_specs=pl.BlockSpec((1, n_new, MLA_HEADS * V_DIM), lambda i: (i, 0, 0)),
        out_shape=jax.ShapeDtypeStruct((nb, n_new, MLA_HEADS * V_DIM), BF16),
        compiler_params=pltpu.CompilerParams(dimension_semantics=("arbitrary",),
                                             vmem_limit_bytes=VMEM_LIMIT),
        name="sample_attention",
    )(qn, qr, ckv_new, kdup, cache_ckv, krc, wuk, wuv)


def _out_ffn_kernel(x_ref, ga_ref, gb_ref, attn_ref, prev_ref, wb_ref, wo_ref, fg_ref, wup_ref, cw_ref, cb_ref,
                    wdn_ref, fing_ref, y_ref, conv_ref, carry_ref, *, seg_rows, tiles_per_seq):
    tm = x_ref.shape[0]
    nseg = tm // seg_rows
    merged = ga_ref[...].astype(F32) + gb_ref[...].astype(F32) * _dot(attn_ref[...], wb_ref[...])
    h = x_ref[...] + _dot(merged.astype(BF16), wo_ref[...])
    hn = _rms(h, fg_ref[...]).astype(BF16)

    if tiles_per_seq == 1:
        prev = prev_ref[...]
    else:
        first = (pl.program_id(0) % tiles_per_seq) == 0
        prev = jnp.where(first, prev_ref[...], carry_ref[...][None])

    row = lax.broadcasted_iota(jnp.int32, (SUBLANES, FF_CHUNK), 0)

    def shifted(u, p0, p1):
        r1 = pltpu.roll(u, 1, 0)
        r2 = pltpu.roll(u, 2, 0)
        top1 = jnp.where(row == 0, p1, r1[:SUBLANES])
        top2 = jnp.where(row == 0, p0, jnp.where(row == 1, p1, r2[:SUBLANES]))
        return (jnp.concatenate([top1, r1[SUBLANES:]], axis=0), jnp.concatenate([top2, r2[SUBLANES:]], axis=0))

    def conv(up, c0):
        cols = slice(c0, c0 + FF_CHUNK)
        w = cw_ref[:, cols]
        outs = []
        for s in range(nseg):
            u = up[s * seg_rows:(s + 1) * seg_rows]
            d1, d2 = shifted(u, prev[s, 0:1, cols], prev[s, 1:2, cols])
            outs.append(cb_ref[:, cols] + d2 * w[0:1] + d1 * w[1:2] + u * w[2:3])
            tail = u[seg_rows - (CONV_W - 1):]
            conv_ref[s, :, cols] = tail
            if tiles_per_seq > 1:
                carry_ref[:, cols] = tail
        return outs[0] if nseg == 1 else jnp.concatenate(outs, axis=0)

    acc = jnp.zeros((tm, D_MODEL), F32)
    for j in range(N_FF_CHUNKS):
        c_val = j * FF_CHUNK
        c_gate = D_FF + j * FF_CHUNK
        val = conv(_dot(hn, wup_ref[:, c_val:c_val + FF_CHUNK]), c_val)
        gate = conv(_dot(hn, wup_ref[:, c_gate:c_gate + FF_CHUNK]), c_gate)
        act = (jax.nn.silu(gate) * val).astype(BF16)
        acc = acc + _dot(act, wdn_ref[c_val:c_val + FF_CHUNK, :])
    y_ref[...] = _rms(h + acc, fing_ref[...])


def _out_ffn(x2d, ga, gb, attn, prev, wb, wo, ffn_g, wup, conv_w, conv_b, wdn, fin_g, *, seg_rows, tm):
    n = x2d.shape[0]
    nseg = tm // seg_rows if seg_rows < tm else 1
    seg = min(seg_rows, tm)
    tiles_per_seq = max(seg_rows // tm, 1)
    row_spec = pl.BlockSpec((tm, D_MODEL), lambda i: (i, 0))
    state_spec = pl.BlockSpec((nseg, CONV_W - 1, 2 * D_FF),
                              (lambda i: (i // tiles_per_seq, 0, 0)) if tiles_per_seq > 1 else (lambda i: (i, 0, 0)))
    return pl.pallas_call(
        functools.partial(_out_ffn_kernel, seg_rows=seg, tiles_per_seq=tiles_per_seq),
        grid=(n // tm,),
        in_specs=[row_spec, row_spec, row_spec, row_spec, state_spec,
                  _resident(wb.shape), _resident(wo.shape), _resident(ffn_g.shape), _resident(wup.shape),
                  _resident(conv_w.shape), _resident(conv_b.shape), _resident(wdn.shape),
                  _resident(fin_g.shape)],
        out_specs=[row_spec, state_spec],
        out_shape=[jax.ShapeDtypeStruct((n, D_MODEL), F32),
                   jax.ShapeDtypeStruct(prev.shape, F32)],
        scratch_shapes=[pltpu.VMEM((CONV_W - 1, 2 * D_FF), F32)],
        compiler_params=pltpu.CompilerParams(dimension_semantics=("arbitrary",),
                                             vmem_limit_bytes=VMEM_LIMIT),
        name="out_ffn",
    )(x2d, ga, gb, attn, prev, wb, wo, ffn_g, wup, conv_w, conv_b, wdn, fin_g)


def _rope_tables(pos):
    inv_freq = ROPE_BASE ** (-jnp.arange(HALF_ROPE, dtype=F32) / HALF_ROPE)
    ang = pos.astype(F32)[:, None] * inv_freq[None, :]
    reps = LANES // HALF_ROPE
    return jnp.tile(jnp.cos(ang), (1, reps)), jnp.tile(jnp.sin(ang), (1, reps))


def _rotated_cols(w):
    return jnp.concatenate([-w[..., HALF_ROPE:], w[..., :HALF_ROPE]], axis=-1)


def kernel(x_prompt, x_sample, cache_mla_ckv, cache_mla_krope, state_ffn_conv, norm_mix_g, w_in, gmlp_ln_g,
           gmlp_ln_b, gmlp_w_s, gmlp_b_s, mla_q_norm_g, mla_w_uq, mla_kv_norm_g, mla_w_uk, mla_w_uv,
           w_proj_a, w_proj_b, w_out, norm_ffn_g, ffn_w_up, ffn_conv_w, ffn_conv_b, ffn_w_down,
           final_norm_g):
    assert w_in.shape[0] == 1, "single-layer step"
    bp, sp, _ = x_prompt.shape
    bs, ss, _ = x_sample.shape
    past = cache_mla_ckv.shape[2]
    assert sp % GMLP_CHUNK == 0 and ss <= GMLP_CHUNK and GMLP_CHUNK % ss == 0

    w0 = w_in[0]
    wuv_in = w0[:, :OFF_Q].astype(BF16)
    wgate = w0[:, OFF_GATE:].astype(BF16)
    w_kr = w0[:, OFF_KV + KV_LORA:OFF_GATE]
    w_kr_rot = _rotated_cols(w_kr)
    wmla = jnp.concatenate([w0[:, OFF_Q:OFF_KV + KV_LORA], w_kr, w_kr, w_kr_rot, w_kr_rot], axis=1).astype(BF16)
    uq = mla_w_uq[0]
    uq_rope = uq[:, :, NOPE_DIM:]
    wuq = jnp.concatenate([uq[:, :, :NOPE_DIM].reshape(Q_LORA, -1), uq_rope.reshape(Q_LORA, -1),
                           _rotated_cols(uq_rope).reshape(Q_LORA, -1)], axis=1).astype(BF16)
    wuk = mla_w_uk[0].reshape(KV_LORA, -1).astype(BF16)
    wuv = mla_w_uv[0].reshape(KV_LORA, -1).astype(BF16)
    wa = w_proj_a[0].astype(BF16)
    wb = w_proj_b[0].astype(BF16)
    wo = w_out[0].astype(BF16)
    wup = ffn_w_up[0].astype(BF16)
    wdn = ffn_w_down[0].astype(BF16)
    conv_w = ffn_conv_w[0]
    conv_b = ffn_conv_b

    def spatial_params(n):
        reps = GMLP_CHUNK // n
        ws = jnp.tile(gmlp_w_s[0][:, :n, :n], (1, reps, reps))
        bias = jnp.tile(jnp.repeat(gmlp_b_s[0][:, :n].T, GMLP_GROUP_DIM, axis=1), (reps, 1))
        return ws, bias

    def run_stream(x, pos_row, n_spatial, prev_conv, *, is_prompt):
        b, s, _ = x.shape
        x2d = x.reshape(b * s, D_MODEL)
        ws, bias = spatial_params(n_spatial)
        res = _gmlp_gate(x2d, norm_mix_g, wuv_in, wgate, gmlp_ln_g, gmlp_ln_b, ws, bias, wa,
                         seg=n_spatial, want_v=not is_prompt, tm=PREP_ROWS)
        cos_t, sin_t = _rope_tables(pos_row)
        prep = _mla_prep(x2d, norm_mix_g, wmla, mla_q_norm_g, wuq, mla_kv_norm_g, cos_t, sin_t, wuk,
                         wuv.T if is_prompt else wuv, expand=is_prompt, tm=PREP_ROWS, seq=s, kv_tile=FLASH_ROWS)
        if is_prompt:
            ga, gb = res
            qn, qr, ckv, kro, kn, kr2, vt = prep
            r3 = lambda a: a.reshape(b, s, a.shape[-1])
            attn = _flash_attention(r3(qn), r3(qr), r3(kn), r3(kr2), vt, t=FLASH_ROWS)
        else:
            ga, gb, v_rows = res
            qn, qr, ckv, kro, kdup = prep
            r3 = lambda a: a.reshape(b, s, a.shape[-1])
            krc = jnp.concatenate([cache_mla_krope[0], cache_mla_krope[0]], axis=-1).astype(BF16)
            k_pos = np.arange(past + s)
            q_pos = past + np.arange(s)
            need_mask = not bool(np.all((k_pos[None, :] // CHUNK) <= (q_pos[:, None] // CHUNK)))
            attn = _sample_attention(r3(qn), r3(qr), r3(ckv), r3(kdup), cache_mla_ckv, krc, wuk, wuv,
                                     need_mask=need_mask)
        y, conv_rows = _out_ffn(x2d, ga, gb, attn.reshape(b * s, -1), prev_conv, wb, wo, norm_ffn_g, wup,
                                conv_w, conv_b, wdn, final_norm_g[None, :], seg_rows=s, tm=FFN_ROWS)
        outs = (y.reshape(b, s, D_MODEL), ckv.reshape(1, b, s, KV_LORA), kro.reshape(1, b, s, ROPE_DIM),
                conv_rows[None])
        if not is_prompt:
            outs = outs + (v_rows.reshape(1, b, s, GMLP_WIDTH),)
        return outs

    pos_p = jnp.arange(sp, dtype=jnp.int32)
    pos_s = jnp.tile(past + jnp.arange(ss, dtype=jnp.int32), (max(PREP_ROWS // ss, 1),))
    conv_zero = jnp.zeros((bp, CONV_W - 1, 2 * D_FF), x_prompt.dtype)

    y_p, ckv_p, kr_p, cv_p = run_stream(x_prompt, pos_p, GMLP_CHUNK, conv_zero, is_prompt=True)
    y_s, ckv_s, kr_s, cv_s, gv_s = run_stream(x_sample, pos_s, ss, state_ffn_conv[0], is_prompt=False)
    return (y_p, y_s, ckv_p, kr_p, cv_p, ckv_s, kr_s, cv_s, gv_s)
```

```python
import functools

import numpy as np
import jax
import jax.numpy as jnp
from jax import lax
from jax.experimental import pallas as pl
from jax.experimental.pallas import tpu as pltpu

D_MODEL = 1024
CHUNK = 64
GMLP_CHUNK = 128
GMLP_GROUPS = 8
GMLP_WIDTH = 1024
GMLP_GROUP_DIM = GMLP_WIDTH // GMLP_GROUPS
MLA_HEADS = 8
Q_LORA = 512
KV_LORA = 512
NOPE_DIM = 128
ROPE_DIM = 64
V_DIM = 128
QK_DIM = NOPE_DIM + ROPE_DIM
ROPE_BASE = 10000.0
ATTN_SCALE = QK_DIM ** -0.5
NEG_INF = -1e30
LOG2_E = 1.4426950408889634
D_FF = 2816
CONV_W = 3
EPS = 1e-6
OFF_Q = 2 * GMLP_WIDTH
OFF_KV = OFF_Q + Q_LORA
OFF_GATE = OFF_KV + KV_LORA + ROPE_DIM

LANES = 128
SUBLANES = 8
HALF_ROPE = ROPE_DIM // 2
V_PAD = 16
FF_CHUNK = 256
N_FF_CHUNKS = D_FF // FF_CHUNK
VMEM_LIMIT = 56 * 1024 * 1024
PREP_ROWS = 512
FLASH_ROWS = 1024
FFN_ROWS = 512
SAMPLE_KEY_CHUNK = 1024

BF16 = jnp.bfloat16
F32 = jnp.float32


def _rms(x, g):
    return x * lax.rsqrt(jnp.mean(x * x, axis=-1, keepdims=True) + EPS) * g


def _dot(a, b):
    return jnp.dot(a, b, preferred_element_type=F32)


def _dot_nt(a, b):
    return lax.dot_general(a, b, (((1,), (1,)), ((), ())), preferred_element_type=F32)


def _resident(shape):
    nd = len(shape)
    return pl.BlockSpec(shape, lambda *_: (0,) * nd, pipeline_mode=pl.Buffered(1))


def _gmlp_gate_kernel(x_ref, g_ref, wuv_ref, wgate_ref, lng_ref, lnb_ref, ws_ref, bs_ref, wa_ref,
                      ga_ref, gb_ref, *maybe_v_ref, seg):
    tm = x_ref.shape[0]
    xn = _rms(x_ref[...], g_ref[...]).astype(BF16)
    ge = jax.nn.gelu(_dot(xn, wuv_ref[...]))
    u = ge[:, :GMLP_WIDTH]
    v = ge[:, GMLP_WIDTH:]
    vc = v - jnp.mean(v, axis=-1, keepdims=True)
    var = jnp.mean(vc * vc, axis=-1, keepdims=True)
    v = vc * lax.rsqrt(var + EPS) * lng_ref[...] + lnb_ref[...]
    if maybe_v_ref:
        maybe_v_ref[0][...] = v
    vb = v.astype(BF16)

    row = lax.broadcasted_iota(jnp.int32, (GMLP_CHUNK, GMLP_CHUNK), 0)
    col = lax.broadcasted_iota(jnp.int32, (GMLP_CHUNK, GMLP_CHUNK), 1)
    keep = (col <= row) & ((col // seg) == (row // seg))
    wm = [jnp.where(keep, ws_ref[g], 0.0).astype(BF16) for g in range(GMLP_GROUPS)]
    bias = bs_ref[...]
    blocks = []
    for c in range(tm // GMLP_CHUNK):
        rows = slice(c * GMLP_CHUNK, (c + 1) * GMLP_CHUNK)
        mix = [_dot(wm[g], vb[rows, g * GMLP_GROUP_DIM:(g + 1) * GMLP_GROUP_DIM])
               for g in range(GMLP_GROUPS)]
        blocks.append(u[rows] * (jnp.concatenate(mix, axis=1) + bias))
    a_out = jnp.concatenate(blocks, axis=0).astype(BF16)
    pa = _dot(a_out, wa_ref[...])
    gates = jax.nn.sigmoid(_dot(xn, wgate_ref[...]))
    ga_ref[...] = (gates[:, :D_MODEL] * pa).astype(BF16)
    gb_ref[...] = gates[:, D_MODEL:].astype(BF16)


def _gmlp_gate(x2d, norm_g, wuv, wgate, ln_g, ln_b, ws, bias_full, wa, *, seg, want_v, tm):
    n = x2d.shape[0]
    row_spec = pl.BlockSpec((tm, D_MODEL), lambda i: (i, 0))
    out_shape = [jax.ShapeDtypeStruct((n, D_MODEL), BF16), jax.ShapeDtypeStruct((n, D_MODEL), BF16)]
    out_specs = [row_spec, row_spec]
    if want_v:
        out_shape.append(jax.ShapeDtypeStruct((n, GMLP_WIDTH), F32))
        out_specs.append(row_spec)
    return pl.pallas_call(
        functools.partial(_gmlp_gate_kernel, seg=seg),
        grid=(n // tm,),
        in_specs=[row_spec, _resident(norm_g.shape), _resident(wuv.shape), _resident(wgate.shape),
                  _resident(ln_g.shape), _resident(ln_b.shape), _resident(ws.shape),
                  _resident(bias_full.shape), _resident(wa.shape)],
        out_specs=out_specs,
        out_shape=out_shape,
        compiler_params=pltpu.CompilerParams(dimension_semantics=("arbitrary",),
                                             vmem_limit_bytes=VMEM_LIMIT),
        name="gmlp_gate",
    )(x2d, norm_g, wuv, wgate, ln_g, ln_b, ws, bias_full, wa)


def _mla_prep_kernel(x_ref, g_ref, wmla_ref, qg_ref, wuq_ref, kvg_ref, cos_ref, sin_ref, *rest, expand):
    if expand:
        wuk_ref, wuv_ref, qn_ref, qr_ref, ckv_ref, kro_ref, kn_ref, kr2_ref, vv_ref = rest
    else:
        qn_ref, qr_ref, ckv_ref, kro_ref, kdup_ref = rest
    xn = _rms(x_ref[...], g_ref[...]).astype(BF16)
    z = _dot(xn, wmla_ref[...])
    q_lat = z[:, :Q_LORA]
    ckv_raw = z[:, Q_LORA:Q_LORA + KV_LORA]
    k_main = z[:, Q_LORA + KV_LORA:Q_LORA + KV_LORA + LANES]
    k_rot = z[:, Q_LORA + KV_LORA + LANES:]
    cos = cos_ref[...]
    sin = sin_ref[...]

    q = _dot(_rms(q_lat, qg_ref[...]).astype(BF16), wuq_ref[...])
    n_nope = MLA_HEADS * NOPE_DIM
    n_rope = MLA_HEADS * ROPE_DIM
    reps = n_rope // LANES
    cos_q = jnp.concatenate([cos] * reps, axis=1)
    sin_q = jnp.concatenate([sin] * reps, axis=1)
    q_scale = ATTN_SCALE * LOG2_E if expand else ATTN_SCALE
    qn_ref[...] = (q[:, :n_nope] * q_scale).astype(BF16)
    q_rope = q[:, n_nope:n_nope + n_rope] * cos_q + q[:, n_nope + n_rope:] * sin_q
    qr_ref[...] = (q_rope * q_scale).astype(BF16)

    c_kv = _rms(ckv_raw, kvg_ref[...])
    ckv_ref[...] = c_kv
    k_dup = k_main * cos + k_rot * sin
    kro_ref[...] = k_dup[:, :ROPE_DIM]
    if expand:
        cb = c_kv.astype(BF16)
        kn_ref[...] = _dot(cb, wuk_ref[...]).astype(BF16)
        pad_row = lax.broadcasted_iota(jnp.int32, (V_PAD, cb.shape[0]), 0)
        ones_rows = jnp.where(pad_row == 0, 1.0, 0.0).astype(BF16)
        for h in range(MLA_HEADS):
            vv_ref[0, h, 0, :V_DIM, :] = _dot_nt(wuv_ref[h * V_DIM:(h + 1) * V_DIM, :], cb).astype(BF16)
            vv_ref[0, h, 0, V_DIM:, :] = ones_rows
        lane = lax.broadcasted_iota(jnp.int32, k_dup.shape, 1)
        kr2_ref[:, :LANES] = jnp.where(lane < ROPE_DIM, k_dup, 0.0).astype(BF16)
        kr2_ref[:, LANES:] = jnp.where(lane >= ROPE_DIM, k_dup, 0.0).astype(BF16)
    else:
        kdup_ref[...] = k_dup.astype(BF16)


def _mla_prep(x2d, norm_g, wmla, q_g, wuq, kv_g, cos_t, sin_t, wuk, wuv, *, expand, tm, seq, kv_tile):
    n = x2d.shape[0]
    table_blocks = cos_t.shape[0] // tm

    def rows(width):
        return pl.BlockSpec((tm, width), lambda i: (i, 0))

    table_spec = pl.BlockSpec((tm, LANES), lambda i: (i % table_blocks, 0))

    n_nope = MLA_HEADS * NOPE_DIM
    n_rope = MLA_HEADS * ROPE_DIM
    in_arrays = [x2d, norm_g, wmla, q_g, wuq, kv_g, cos_t, sin_t]
    in_specs = [rows(D_MODEL), _resident(norm_g.shape), _resident(wmla.shape), _resident(q_g.shape),
                _resident(wuq.shape), _resident(kv_g.shape), table_spec, table_spec]
    out_shape = [jax.ShapeDtypeStruct((n, n_nope), BF16), jax.ShapeDtypeStruct((n, n_rope), BF16),
                 jax.ShapeDtypeStruct((n, KV_LORA), F32), jax.ShapeDtypeStruct((n, ROPE_DIM), F32)]
    out_specs = [rows(n_nope), rows(n_rope), rows(KV_LORA), rows(ROPE_DIM)]
    if expand:
        in_arrays += [wuk, wuv]
        in_specs += [_resident(wuk.shape), _resident(wuv.shape)]
        tiles_per_seq = seq // tm
        sub = kv_tile // tm
        vt_spec = pl.BlockSpec(
            (1, MLA_HEADS, 1, V_DIM + V_PAD, tm),
            lambda i: (i // tiles_per_seq, 0, (i % tiles_per_seq) // sub, 0, (i % tiles_per_seq) % sub))
        out_shape += [jax.ShapeDtypeStruct((n, n_nope), BF16), jax.ShapeDtypeStruct((n, 2 * LANES), BF16),
                      jax.ShapeDtypeStruct((n // seq, MLA_HEADS, seq // kv_tile, V_DIM + V_PAD, kv_tile), BF16)]
        out_specs += [rows(n_nope), rows(2 * LANES), vt_spec]
    else:
        out_shape += [jax.ShapeDtypeStruct((n, LANES), BF16)]
        out_specs += [rows(LANES)]
    return pl.pallas_call(
        functools.partial(_mla_prep_kernel, expand=expand),
        grid=(n // tm,),
        in_specs=in_specs,
        out_specs=out_specs,
        out_shape=out_shape,
        compiler_params=pltpu.CompilerParams(dimension_semantics=("arbitrary",),
                                             vmem_limit_bytes=VMEM_LIMIT),
        name="mla_prep",
    )(*in_arrays)


def _flash_kernel(qn_ref, qr_ref, kn_ref, kr_ref, vt_ref, o_ref, s0_ref, s1_ref, p0_ref, p1_ref, bm0_ref, bm1_ref,
                  a0_ref, a1_ref, m_ref, acc_ref, *, t):
    s_ref, p_ref, bm_ref, a_ref = (s0_ref, s1_ref), (p0_ref, p1_ref), (bm0_ref, bm1_ref), (a0_ref, a1_ref)
    n = pl.program_id(2) + 1
    m_ref[...] = jnp.full(m_ref.shape, -jnp.inf, F32)
    acc_ref[...] = jnp.zeros(acc_ref.shape, F32)

    def scores(j, slot, masked=False):
        start = pl.multiple_of(j * t, t)
        k = jnp.concatenate([kn_ref[0, pl.ds(start, t), :], kr_ref[0, pl.ds(start, t), :]], axis=1)
        q = jnp.concatenate([qn_ref[0], qr_ref[0]], axis=1)
        st = _dot_nt(k, q)
        if masked:
            key = lax.broadcasted_iota(jnp.int32, st.shape, 0)
            qry = lax.broadcasted_iota(jnp.int32, st.shape, 1)
            st = jnp.where((key // CHUNK) <= (qry // CHUNK), st, NEG_INF)
        s_ref[slot][...] = st
        bm_ref[slot][...] = jnp.max(st, axis=0, keepdims=True)

    def softmax(slot):
        m_old = m_ref[...]
        m_new = jnp.maximum(m_old, bm_ref[slot][...])
        alpha = jnp.exp2(m_old - m_new)
        p_ref[slot][...] = jnp.exp2(s_ref[slot][...] - m_new).astype(BF16)
        a_ref[slot][...] = alpha
        m_ref[...] = m_new

    def values(j, slot):
        acc_ref[...] = a_ref[slot][...] * acc_ref[...] + _dot(vt_ref[0, 0, j], p_ref[slot][...])

    @pl.when(n <= 2)
    def _():
        @pl.when(n == 2)
        def _():
            scores(0, 0)
            softmax(0)
            values(0, 0)
        scores(n - 1, 1, masked=True)
        softmax(1)
        values(n - 1, 1)

    @pl.when(n >= 3)
    def _():
        scores(0, 0)
        scores(1, 1)
        softmax(0)

    def pair(c, carry):
        i = 2 * c + 1
        scores(i + 1, 0)
        softmax(1)
        values(i - 1, 0)
        scores(i + 2, 1)
        softmax(0)
        values(i, 1)
        return carry

    lax.fori_loop(0, jnp.maximum(n - 3, 0) // 2, pair, 0)

    @pl.when((n >= 3) & (n % 2 == 1))
    def _():
        scores(n - 1, 0, masked=True)
        softmax(1)
        values(n - 3, 0)
        softmax(0)
        values(n - 2, 1)
        values(n - 1, 0)

    @pl.when((n >= 3) & (n % 2 == 0))
    def _():
        scores(n - 2, 0)
        softmax(1)
        values(n - 4, 0)
        scores(n - 1, 1, masked=True)
        softmax(0)
        values(n - 3, 1)
        softmax(1)
        values(n - 2, 0)
        values(n - 1, 1)

    o_ref[0] = (acc_ref[:V_DIM, :] / acc_ref[V_DIM:V_DIM + 1, :]).T.astype(o_ref.dtype)


def _flash_attention(qn, qr, kn, kr2, vt, *, t):
    b, s, _ = qn.shape
    n_kb = s // t
    return pl.pallas_call(
        functools.partial(_flash_kernel, t=t),
        grid=(b, MLA_HEADS, s // t),
        in_specs=[
            pl.BlockSpec((1, t, LANES), lambda bi, h, qi: (bi, qi, h)),
            pl.BlockSpec((1, t, LANES), lambda bi, h, qi: (bi, qi, h // 2)),
            pl.BlockSpec((1, s, LANES), lambda bi, h, qi: (bi, 0, h)),
            pl.BlockSpec((1, s, LANES), lambda bi, h, qi: (bi, 0, h % 2)),
            pl.BlockSpec((1, 1, n_kb, V_DIM + V_PAD, t), lambda bi, h, qi: (bi, h, 0, 0, 0)),
        ],
        out_specs=pl.BlockSpec((1, t, LANES), lambda bi, h, qi: (bi, qi, h)),
        out_shape=jax.ShapeDtypeStruct((b, s, MLA_HEADS * V_DIM), BF16),
        scratch_shapes=[pltpu.VMEM((t, t), F32), pltpu.VMEM((t, t), F32),
                        pltpu.VMEM((t, t), BF16), pltpu.VMEM((t, t), BF16),
                        pltpu.VMEM((1, t), F32), pltpu.VMEM((1, t), F32),
                        pltpu.VMEM((1, t), F32), pltpu.VMEM((1, t), F32),
                        pltpu.VMEM((1, t), F32), pltpu.VMEM((V_DIM + V_PAD, t), F32)],
        compiler_params=pltpu.CompilerParams(dimension_semantics=("arbitrary", "arbitrary", "arbitrary"),
                                             vmem_limit_bytes=VMEM_LIMIT),
        name="flash_attention",
    )(qn, qr, kn, kr2, vt)


def _sample_attn_kernel(qn_ref, qr_ref, ckvn_ref, kdup_ref, cache_ref, krc_ref, wuk_ref, wuv_ref, o_ref,
                        *, past, n_new, need_mask, key_chunk):
    qn = qn_ref[0]
    qr = qr_ref[0]
    lane = lax.broadcasted_iota(jnp.int32, (n_new, LANES), 1)
    q_lat, q_rope = [], []
    for h in range(MLA_HEADS):
        q_lat.append(_dot_nt(qn[:, h * NOPE_DIM:(h + 1) * NOPE_DIM],
                             wuk_ref[:, h * NOPE_DIM:(h + 1) * NOPE_DIM]).astype(BF16))
        pair = qr[:, (h // 2) * LANES:(h // 2 + 1) * LANES]
        half = (lane < ROPE_DIM) if h % 2 == 0 else (lane >= ROPE_DIM)
        q_rope.append(jnp.where(half, pair, jnp.zeros_like(pair)))
    q_all = jnp.concatenate([jnp.concatenate(q_lat, axis=0), jnp.concatenate(q_rope, axis=0)], axis=1)

    rows = MLA_HEADS * n_new
    q_pos = past + lax.broadcasted_iota(jnp.int32, (rows, 1), 0) % n_new

    def chunk_keys(c):
        if c < past // key_chunk:
            ks = slice(c * key_chunk, (c + 1) * key_chunk)
            kr = krc_ref[0, 0, ks, :].astype(BF16)
            return (jnp.concatenate([cache_ref[0, 0, ks, :].astype(BF16), kr, kr], axis=1),
                    c * key_chunk, key_chunk)
        pad = LANES - n_new
        c_new = jnp.concatenate([ckvn_ref[0].astype(BF16), jnp.zeros((pad, KV_LORA), BF16)], axis=0)
        r_new = jnp.concatenate([kdup_ref[0], jnp.zeros((pad, LANES), BF16)], axis=0)
        return jnp.concatenate([c_new, r_new], axis=1), past, n_new

    m = jnp.full((rows, 1), -jnp.inf, F32)
    denom = jnp.zeros((rows, 1), F32)
    acc = jnp.zeros((rows, KV_LORA), F32)
    for c in range(past // key_chunk + 1):
        keys, pos0, n_real = chunk_keys(c)
        s = _dot_nt(q_all, keys)
        col = lax.broadcasted_iota(jnp.int32, s.shape, 1)
        visible = None
        if n_real < keys.shape[0]:
            visible = col < n_real
        if need_mask:
            causal = ((pos0 + col) // CHUNK) <= (q_pos // CHUNK)
            visible = causal if visible is None else (visible & causal)
        if visible is not None:
            s = jnp.where(visible, s, NEG_INF)
        m_new = jnp.maximum(m, jnp.max(s, axis=1, keepdims=True))
        alpha = jnp.exp(m - m_new)
        p = jnp.exp(s - m_new)
        denom = alpha * denom + jnp.sum(p, axis=1, keepdims=True)
        acc = alpha * acc + _dot(p.astype(BF16), keys[:, :KV_LORA])
        m = m_new
    o_lat = (acc / denom).astype(BF16)
    outs = [_dot(o_lat[h * n_new:(h + 1) * n_new], wuv_ref[:, h * V_DIM:(h + 1) * V_DIM])
            for h in range(MLA_HEADS)]
    o_ref[0] = jnp.concatenate(outs, axis=1).astype(o_ref.dtype)


def _sample_attention(qn, qr, ckv_new, kdup, cache_ckv, cache_krope, wuk, wuv, *, need_mask):
    nb, n_new, _ = qn.shape
    past = cache_ckv.shape[2]
    key_chunk = SAMPLE_KEY_CHUNK if past % SAMPLE_KEY_CHUNK == 0 else past
    return pl.pallas_call(
        functools.partial(_sample_attn_kernel, past=past, n_new=n_new, need_mask=need_mask, key_chunk=key_chunk),
        grid=(nb,),
        in_specs=[
            pl.BlockSpec((1, n_new, qn.shape[2]), lambda i: (i, 0, 0)),
            pl.BlockSpec((1, n_new, qr.shape[2]), lambda i: (i, 0, 0)),
            pl.BlockSpec((1, n_new, KV_LORA), lambda i: (i, 0, 0)),
            pl.BlockSpec((1, n_new, LANES), lambda i: (i, 0, 0)),
            pl.BlockSpec((1, 1, past, KV_LORA), lambda i: (0, i, 0, 0)),
            pl.BlockSpec((1, 1, past, ROPE_DIM), lambda i: (0, i, 0, 0)),
            _resident(wuk.shape), _resident(wuv.shape),
        ],
        out_specs=pl.BlockSpec((1, n_new, MLA_HEADS * V_DIM), lambda i: (i, 0, 0)),
        out_shape=jax.ShapeDtypeStruct((nb, n_new, MLA_HEADS * V_DIM), BF16),
        compiler_params=pltpu.CompilerParams(dimension_semantics=("arbitrary",),
                                             vmem_limit_bytes=VMEM_LIMIT),
        name="sample_attention",
    )(qn, qr, ckv_new, kdup, cache_ckv, cache_krope, wuk, wuv)


def _out_ffn_kernel(x_ref, ga_ref, gb_ref, attn_ref, prev_ref, wb_ref, wo_ref, fg_ref, wup_ref, cw_ref, cb_ref,
                    wdn_ref, fing_ref, y_ref, conv_ref, carry_ref, h_ref, hn_ref, acc_ref,
                    upv0_ref, upv1_ref, upg0_ref, upg1_ref, act0_ref, act1_ref, *, seg_rows, tiles_per_seq):
    tm = x_ref.shape[0]
    nseg = tm // seg_rows
    nc = N_FF_CHUNKS
    upv_ref, upg_ref, act_ref = (upv0_ref, upv1_ref), (upg0_ref, upg1_ref), (act0_ref, act1_ref)
    merged = ga_ref[...].astype(F32) + gb_ref[...].astype(F32) * _dot(attn_ref[...], wb_ref[...])
    h = x_ref[...] + _dot(merged.astype(BF16), wo_ref[...])
    h_ref[...] = h
    hn_ref[...] = _rms(h, fg_ref[...]).astype(BF16)
    acc_ref[...] = jnp.zeros(acc_ref.shape, F32)

    row = lax.broadcasted_iota(jnp.int32, (SUBLANES, FF_CHUNK), 0)

    def shifted(u, p0, p1):
        r1 = pltpu.roll(u, 1, 0)
        r2 = pltpu.roll(u, 2, 0)
        top1 = jnp.where(row == 0, p1, r1[:SUBLANES])
        top2 = jnp.where(row == 0, p0, jnp.where(row == 1, p1, r2[:SUBLANES]))
        return (jnp.concatenate([top1, r1[SUBLANES:]], axis=0), jnp.concatenate([top2, r2[SUBLANES:]], axis=0))

    def conv(up, cj):
        w = cw_ref[cj]
        outs = []
        for s in range(nseg):
            u = up[s * seg_rows:(s + 1) * seg_rows]
            prev = prev_ref[s, cj]
            if tiles_per_seq > 1:
                first = (pl.program_id(0) % tiles_per_seq) == 0
                prev = jnp.where(first, prev, carry_ref[cj])
            d1, d2 = shifted(u, prev[0:1], prev[1:2])
            outs.append(cb_ref[cj] + d2 * w[0:1] + d1 * w[1:2] + u * w[2:3])
            tail = u[seg_rows - (CONV_W - 1):]
            conv_ref[s, cj] = tail
            if tiles_per_seq > 1:
                carry_ref[cj] = tail
        return outs[0] if nseg == 1 else jnp.concatenate(outs, axis=0)

    def up_proj(j, slot):
        hn = hn_ref[...]
        upv_ref[slot][...] = _dot(hn, wup_ref[j])
        upg_ref[slot][...] = _dot(hn, wup_ref[nc + j])

    def gate(j, slot):
        val = conv(upv_ref[slot][...], j)
        gat = conv(upg_ref[slot][...], nc + j)
        act_ref[slot][...] = (jax.nn.silu(gat) * val).astype(BF16)

    def down_proj(j, slot):
        acc_ref[...] += _dot(act_ref[slot][...], wdn_ref[j])

    up_proj(0, 0)
    up_proj(1, 1)
    gate(0, 0)

    def pair(c, carry):
        i = 2 * c + 1
        up_proj(i + 1, 0)
        gate(i, 1)
        down_proj(i - 1, 0)
        up_proj(i + 2, 1)
        gate(i + 1, 0)
        down_proj(i, 1)
        return carry

    lax.fori_loop(0, (nc - 3) // 2, pair, 0)
    up_proj(nc - 1, 0)
    gate(nc - 2, 1)
    down_proj(nc - 3, 0)
    gate(nc - 1, 0)
    down_proj(nc - 2, 1)
    down_proj(nc - 1, 0)
    y_ref[...] = _rms(h_ref[...] + acc_ref[...], fing_ref[...])


def _out_ffn(x2d, ga, gb, attn, prev, wb, wo, ffn_g, wup, conv_w, conv_b, wdn, fin_g, *, seg_rows, tm):
    assert N_FF_CHUNKS % 2 == 1 and N_FF_CHUNKS >= 3
    n = x2d.shape[0]
    nseg = tm // seg_rows if seg_rows < tm else 1
    seg = min(seg_rows, tm)
    tiles_per_seq = max(seg_rows // tm, 1)
    row_spec = pl.BlockSpec((tm, D_MODEL), lambda i: (i, 0))
    state_spec = pl.BlockSpec((nseg,) + prev.shape[1:],
                              (lambda i: (i // tiles_per_seq, 0, 0, 0)) if tiles_per_seq > 1
                              else (lambda i: (i, 0, 0, 0)))
    chunk_f32 = pltpu.VMEM((tm, FF_CHUNK), F32)
    return pl.pallas_call(
        functools.partial(_out_ffn_kernel, seg_rows=seg, tiles_per_seq=tiles_per_seq),
        grid=(n // tm,),
        in_specs=[row_spec, row_spec, row_spec, row_spec, state_spec,
                  _resident(wb.shape), _resident(wo.shape), _resident(ffn_g.shape), _resident(wup.shape),
                  _resident(conv_w.shape), _resident(conv_b.shape), _resident(wdn.shape),
                  _resident(fin_g.shape)],
        out_specs=[row_spec, state_spec],
        out_shape=[jax.ShapeDtypeStruct((n, D_MODEL), F32),
                   jax.ShapeDtypeStruct(prev.shape, F32)],
        scratch_shapes=[pltpu.VMEM(prev.shape[1:], F32), pltpu.VMEM((tm, D_MODEL), F32),
                        pltpu.VMEM((tm, D_MODEL), BF16), pltpu.VMEM((tm, D_MODEL), F32),
                        chunk_f32, chunk_f32, chunk_f32, chunk_f32,
                        pltpu.VMEM((tm, FF_CHUNK), BF16), pltpu.VMEM((tm, FF_CHUNK), BF16)],
        compiler_params=pltpu.CompilerParams(dimension_semantics=("arbitrary",),
                                             vmem_limit_bytes=VMEM_LIMIT),
        name="out_ffn",
    )(x2d, ga, gb, attn, prev, wb, wo, ffn_g, wup, conv_w, conv_b, wdn, fin_g)


def _rope_tables(pos):
    inv_freq = ROPE_BASE ** (-jnp.arange(HALF_ROPE, dtype=F32) / HALF_ROPE)
    ang = pos.astype(F32)[:, None] * inv_freq[None, :]
    reps = LANES // HALF_ROPE
    return jnp.tile(jnp.cos(ang), (1, reps)), jnp.tile(jnp.sin(ang), (1, reps))


def _rotated_cols(w):
    return jnp.concatenate([-w[..., HALF_ROPE:], w[..., :HALF_ROPE]], axis=-1)


def kernel(x_prompt, x_sample, cache_mla_ckv, cache_mla_krope, state_ffn_conv, norm_mix_g, w_in, gmlp_ln_g,
           gmlp_ln_b, gmlp_w_s, gmlp_b_s, mla_q_norm_g, mla_w_uq, mla_kv_norm_g, mla_w_uk, mla_w_uv,
           w_proj_a, w_proj_b, w_out, norm_ffn_g, ffn_w_up, ffn_conv_w, ffn_conv_b, ffn_w_down,
           final_norm_g):
    assert w_in.shape[0] == 1, "single-layer step"
    bp, sp, _ = x_prompt.shape
    bs, ss, _ = x_sample.shape
    past = cache_mla_ckv.shape[2]
    assert sp % GMLP_CHUNK == 0 and ss <= GMLP_CHUNK and GMLP_CHUNK % ss == 0

    w0 = w_in[0]
    wuv_in = w0[:, :OFF_Q].astype(BF16)
    wgate = w0[:, OFF_GATE:].astype(BF16)
    w_kr = w0[:, OFF_KV + KV_LORA:OFF_GATE]
    w_kr_rot = _rotated_cols(w_kr)
    wmla = jnp.concatenate([w0[:, OFF_Q:OFF_KV + KV_LORA], w_kr, w_kr, w_kr_rot, w_kr_rot], axis=1).astype(BF16)
    uq = mla_w_uq[0]
    uq_rope = uq[:, :, NOPE_DIM:]
    wuq = jnp.concatenate([uq[:, :, :NOPE_DIM].reshape(Q_LORA, -1), uq_rope.reshape(Q_LORA, -1),
                           _rotated_cols(uq_rope).reshape(Q_LORA, -1)], axis=1).astype(BF16)
    wuk = mla_w_uk[0].reshape(KV_LORA, -1).astype(BF16)
    wuv = mla_w_uv[0].reshape(KV_LORA, -1).astype(BF16)
    wa = w_proj_a[0].astype(BF16)
    wb = w_proj_b[0].astype(BF16)
    wo = w_out[0].astype(BF16)
    wup = ffn_w_up[0].astype(BF16).reshape(D_MODEL, 2 * N_FF_CHUNKS, FF_CHUNK).transpose(1, 0, 2)
    wdn = ffn_w_down[0].astype(BF16).reshape(N_FF_CHUNKS, FF_CHUNK, D_MODEL)
    conv_w = ffn_conv_w[0].reshape(CONV_W, 2 * N_FF_CHUNKS, FF_CHUNK).transpose(1, 0, 2)
    conv_b = ffn_conv_b.reshape(1, 2 * N_FF_CHUNKS, FF_CHUNK).transpose(1, 0, 2)

    def conv_rows_chunked(rows):
        return rows.reshape(rows.shape[0], CONV_W - 1, 2 * N_FF_CHUNKS, FF_CHUNK).transpose(0, 2, 1, 3)

    def conv_rows_flat(rows):
        return rows.transpose(0, 2, 1, 3).reshape(rows.shape[0], CONV_W - 1, 2 * D_FF)

    def spatial_params(n):
        reps = GMLP_CHUNK // n
        ws = jnp.tile(gmlp_w_s[0][:, :n, :n], (1, reps, reps))
        bias = jnp.tile(jnp.repeat(gmlp_b_s[0][:, :n].T, GMLP_GROUP_DIM, axis=1), (reps, 1))
        return ws, bias

    def run_stream(x, pos_row, n_spatial, prev_conv, *, is_prompt):
        b, s, _ = x.shape
        x2d = x.reshape(b * s, D_MODEL)
        ws, bias = spatial_params(n_spatial)
        res = _gmlp_gate(x2d, norm_mix_g, wuv_in, wgate, gmlp_ln_g, gmlp_ln_b, ws, bias, wa,
                         seg=n_spatial, want_v=not is_prompt, tm=PREP_ROWS)
        cos_t, sin_t = _rope_tables(pos_row)
        prep = _mla_prep(x2d, norm_mix_g, wmla, mla_q_norm_g, wuq, mla_kv_norm_g, cos_t, sin_t, wuk,
                         wuv.T if is_prompt else wuv, expand=is_prompt, tm=PREP_ROWS, seq=s, kv_tile=FLASH_ROWS)
        if is_prompt:
            ga, gb = res
            qn, qr, ckv, kro, kn, kr2, vt = prep
            r3 = lambda a: a.reshape(b, s, a.shape[-1])
            attn = _flash_attention(r3(qn), r3(qr), r3(kn), r3(kr2), vt, t=FLASH_ROWS)
        else:
            ga, gb, v_rows = res
            qn, qr, ckv, kro, kdup = prep
            r3 = lambda a: a.reshape(b, s, a.shape[-1])
            k_pos = np.arange(past + s)
            q_pos = past + np.arange(s)
            need_mask = not bool(np.all((k_pos[None, :] // CHUNK) <= (q_pos[:, None] // CHUNK)))
            attn = _sample_attention(r3(qn), r3(qr), r3(ckv), r3(kdup), cache_mla_ckv, cache_mla_krope, wuk, wuv,
                                     need_mask=need_mask)
        y, conv_rows = _out_ffn(x2d, ga, gb, attn.reshape(b * s, -1), conv_rows_chunked(prev_conv), wb, wo,
                                norm_ffn_g, wup, conv_w, conv_b, wdn, final_norm_g[None, :], seg_rows=s,
                                tm=FFN_ROWS)
        outs = (y.reshape(b, s, D_MODEL), ckv.reshape(1, b, s, KV_LORA), kro.reshape(1, b, s, ROPE_DIM),
                conv_rows_flat(conv_rows)[None])
        if not is_prompt:
            outs = outs + (v_rows.reshape(1, b, s, GMLP_WIDTH),)
        return outs

    pos_p = jnp.arange(sp, dtype=jnp.int32)
    pos_s = jnp.tile(past + jnp.arange(ss, dtype=jnp.int32), (max(PREP_ROWS // ss, 1),))
    conv_zero = jnp.zeros((bp, CONV_W - 1, 2 * D_FF), x_prompt.dtype)

    y_p, ckv_p, kr_p, cv_p = run_stream(x_prompt, pos_p, GMLP_CHUNK, conv_zero, is_prompt=True)
    y_s, ckv_s, kr_s, cv_s, gv_s = run_stream(x_sample, pos_s, ss, state_ffn_conv[0], is_prompt=False)
    return (y_p, y_s, ckv_p, kr_p, cv_p, ckv_s, kr_s, cv_s, gv_s)
```

```python
import functools

import numpy as np
import jax
import jax.numpy as jnp
from jax import lax
from jax.experimental import pallas as pl
from jax.experimental.pallas import tpu as pltpu

D_MODEL = 1024
CHUNK = 64
GMLP_CHUNK = 128
GMLP_GROUPS = 8
GMLP_WIDTH = 1024
GMLP_GROUP_DIM = GMLP_WIDTH // GMLP_GROUPS
MLA_HEADS = 8
Q_LORA = 512
KV_LORA = 512
NOPE_DIM = 128
ROPE_DIM = 64
V_DIM = 128
QK_DIM = NOPE_DIM + ROPE_DIM
ROPE_BASE = 10000.0
ATTN_SCALE = QK_DIM ** -0.5
NEG_INF = -1e30
LOG2_E = 1.4426950408889634
D_FF = 2816
CONV_W = 3
EPS = 1e-6
OFF_Q = 2 * GMLP_WIDTH
OFF_KV = OFF_Q + Q_LORA
OFF_GATE = OFF_KV + KV_LORA + ROPE_DIM

LANES = 128
SUBLANES = 8
HALF_ROPE = ROPE_DIM // 2
V_PAD = 16
FF_CHUNK = 256
N_FF_CHUNKS = D_FF // FF_CHUNK
VMEM_LIMIT = 56 * 1024 * 1024
PREP_ROWS = 512
FLASH_ROWS = 1024
FFN_ROWS = 512
SAMPLE_KEY_CHUNK = 1024

BF16 = jnp.bfloat16
F32 = jnp.float32


def _rms(x, g):
    return x * lax.rsqrt(jnp.mean(x * x, axis=-1, keepdims=True) + EPS) * g


def _dot(a, b):
    return jnp.dot(a, b, preferred_element_type=F32)


def _dot_nt(a, b):
    return lax.dot_general(a, b, (((1,), (1,)), ((), ())), preferred_element_type=F32)


def _resident(shape):
    nd = len(shape)
    return pl.BlockSpec(shape, lambda *_: (0,) * nd, pipeline_mode=pl.Buffered(1))


def _gmlp_gate_kernel(x_ref, g_ref, wuv_ref, wgate_ref, lng_ref, lnb_ref, ws_ref, bs_ref, wa_ref,
                      ga_ref, gb_ref, *maybe_v_ref, seg):
    tm = x_ref.shape[0]
    xn = _rms(x_ref[...], g_ref[...]).astype(BF16)
    ge = jax.nn.gelu(_dot(xn, wuv_ref[...]))
    u = ge[:, :GMLP_WIDTH]
    v = ge[:, GMLP_WIDTH:]
    vc = v - jnp.mean(v, axis=-1, keepdims=True)
    var = jnp.mean(vc * vc, axis=-1, keepdims=True)
    v = vc * lax.rsqrt(var + EPS) * lng_ref[...] + lnb_ref[...]
    if maybe_v_ref:
        maybe_v_ref[0][...] = v
    vb = v.astype(BF16)

    row = lax.broadcasted_iota(jnp.int32, (GMLP_CHUNK, GMLP_CHUNK), 0)
    col = lax.broadcasted_iota(jnp.int32, (GMLP_CHUNK, GMLP_CHUNK), 1)
    keep = (col <= row) & ((col // seg) == (row // seg))
    wm = [jnp.where(keep, ws_ref[g], 0.0).astype(BF16) for g in range(GMLP_GROUPS)]
    bias = bs_ref[...]
    n_blocks = tm // GMLP_CHUNK
    mix = []
    for g in range(GMLP_GROUPS):
        cols = slice(g * GMLP_GROUP_DIM, (g + 1) * GMLP_GROUP_DIM)
        rhs = jnp.concatenate([vb[c * GMLP_CHUNK:(c + 1) * GMLP_CHUNK, cols] for c in range(n_blocks)], axis=1)
        mix.append(_dot(wm[g], rhs))
    blocks = []
    for c in range(n_blocks):
        lanes = slice(c * GMLP_GROUP_DIM, (c + 1) * GMLP_GROUP_DIM)
        s_c = jnp.concatenate([mix[g][:, lanes] for g in range(GMLP_GROUPS)], axis=1) + bias
        blocks.append(u[c * GMLP_CHUNK:(c + 1) * GMLP_CHUNK] * s_c)
    a_out = jnp.concatenate(blocks, axis=0).astype(BF16)
    pa = _dot(a_out, wa_ref[...])
    gates = jax.nn.sigmoid(_dot(xn, wgate_ref[...]))
    ga_ref[...] = (gates[:, :D_MODEL] * pa).astype(BF16)
    gb_ref[...] = gates[:, D_MODEL:].astype(BF16)


def _gmlp_gate(x2d, norm_g, wuv, wgate, ln_g, ln_b, ws, bias_full, wa, *, seg, want_v, tm):
    n = x2d.shape[0]
    row_spec = pl.BlockSpec((tm, D_MODEL), lambda i: (i, 0))
    out_shape = [jax.ShapeDtypeStruct((n, D_MODEL), BF16), jax.ShapeDtypeStruct((n, D_MODEL), BF16)]
    out_specs = [row_spec, row_spec]
    if want_v:
        out_shape.append(jax.ShapeDtypeStruct((n, GMLP_WIDTH), F32))
        out_specs.append(row_spec)
    return pl.pallas_call(
        functools.partial(_gmlp_gate_kernel, seg=seg),
        grid=(n // tm,),
        in_specs=[row_spec, _resident(norm_g.shape), _resident(wuv.shape), _resident(wgate.shape),
                  _resident(ln_g.shape), _resident(ln_b.shape), _resident(ws.shape),
                  _resident(bias_full.shape), _resident(wa.shape)],
        out_specs=out_specs,
        out_shape=out_shape,
        compiler_params=pltpu.CompilerParams(dimension_semantics=("arbitrary",),
                                             vmem_limit_bytes=VMEM_LIMIT),
        name="gmlp_gate",
    )(x2d, norm_g, wuv, wgate, ln_g, ln_b, ws, bias_full, wa)


def _mla_prep_kernel(x_ref, g_ref, wmla_ref, qg_ref, wuq_ref, kvg_ref, cos_ref, sin_ref, *rest, expand):
    if expand:
        wuk_ref, wuv_ref, qn_ref, qr_ref, ckv_ref, kro_ref, kn_ref, kr2_ref, vv_ref = rest
    else:
        qn_ref, qr_ref, ckv_ref, kro_ref, kdup_ref = rest
    xn = _rms(x_ref[...], g_ref[...]).astype(BF16)
    z = _dot(xn, wmla_ref[...])
    q_lat = z[:, :Q_LORA]
    ckv_raw = z[:, Q_LORA:Q_LORA + KV_LORA]
    k_main = z[:, Q_LORA + KV_LORA:Q_LORA + KV_LORA + LANES]
    k_rot = z[:, Q_LORA + KV_LORA + LANES:]
    cos = cos_ref[...]
    sin = sin_ref[...]

    q = _dot(_rms(q_lat, qg_ref[...]).astype(BF16), wuq_ref[...])
    n_nope = MLA_HEADS * NOPE_DIM
    n_rope = MLA_HEADS * ROPE_DIM
    reps = n_rope // LANES
    cos_q = jnp.concatenate([cos] * reps, axis=1)
    sin_q = jnp.concatenate([sin] * reps, axis=1)
    q_scale = ATTN_SCALE * LOG2_E if expand else ATTN_SCALE
    qn_ref[...] = (q[:, :n_nope] * q_scale).astype(BF16)
    q_rope = q[:, n_nope:n_nope + n_rope] * cos_q + q[:, n_nope + n_rope:] * sin_q
    qr_ref[...] = (q_rope * q_scale).astype(BF16)

    c_kv = _rms(ckv_raw, kvg_ref[...])
    ckv_ref[...] = c_kv
    k_dup = k_main * cos + k_rot * sin
    kro_ref[...] = k_dup[:, :ROPE_DIM]
    if expand:
        cb = c_kv.astype(BF16)
        kn_ref[...] = _dot(cb, wuk_ref[...]).astype(BF16)
        pad_row = lax.broadcasted_iota(jnp.int32, (V_PAD, cb.shape[0]), 0)
        ones_rows = jnp.where(pad_row == 0, 1.0, 0.0).astype(BF16)
        v_t = _dot_nt(wuv_ref[...], cb).astype(BF16)
        for h in range(MLA_HEADS):
            vv_ref[0, h, 0, :V_DIM, :] = v_t[h * V_DIM:(h + 1) * V_DIM]
            vv_ref[0, h, 0, V_DIM:, :] = ones_rows
        lane = lax.broadcasted_iota(jnp.int32, k_dup.shape, 1)
        kr2_ref[:, :LANES] = jnp.where(lane < ROPE_DIM, k_dup, 0.0).astype(BF16)
        kr2_ref[:, LANES:] = jnp.where(lane >= ROPE_DIM, k_dup, 0.0).astype(BF16)
    else:
        kdup_ref[...] = k_dup.astype(BF16)


def _mla_prep(x2d, norm_g, wmla, q_g, wuq, kv_g, cos_t, sin_t, wuk, wuv, *, expand, tm, seq, kv_tile):
    n = x2d.shape[0]
    table_blocks = cos_t.shape[0] // tm

    def rows(width):
        return pl.BlockSpec((tm, width), lambda i: (i, 0))

    table_spec = pl.BlockSpec((tm, LANES), lambda i: (i % table_blocks, 0))

    n_nope = MLA_HEADS * NOPE_DIM
    n_rope = MLA_HEADS * ROPE_DIM
    in_arrays = [x2d, norm_g, wmla, q_g, wuq, kv_g, cos_t, sin_t]
    in_specs = [rows(D_MODEL), _resident(norm_g.shape), _resident(wmla.shape), _resident(q_g.shape),
                _resident(wuq.shape), _resident(kv_g.shape), table_spec, table_spec]
    out_shape = [jax.ShapeDtypeStruct((n, n_nope), BF16), jax.ShapeDtypeStruct((n, n_rope), BF16),
                 jax.ShapeDtypeStruct((n, KV_LORA), F32), jax.ShapeDtypeStruct((n, ROPE_DIM), F32)]
    out_specs = [rows(n_nope), rows(n_rope), rows(KV_LORA), rows(ROPE_DIM)]
    if expand:
        in_arrays += [wuk, wuv]
        in_specs += [_resident(wuk.shape), _resident(wuv.shape)]
        tiles_per_seq = seq // tm
        sub = kv_tile // tm
        vt_spec = pl.BlockSpec(
            (1, MLA_HEADS, 1, V_DIM + V_PAD, tm),
            lambda i: (i // tiles_per_seq, 0, (i % tiles_per_seq) // sub, 0, (i % tiles_per_seq) % sub))
        out_shape += [jax.ShapeDtypeStruct((n, n_nope), BF16), jax.ShapeDtypeStruct((n, 2 * LANES), BF16),
                      jax.ShapeDtypeStruct((n // seq, MLA_HEADS, seq // kv_tile, V_DIM + V_PAD, kv_tile), BF16)]
        out_specs += [rows(n_nope), rows(2 * LANES), vt_spec]
    else:
        out_shape += [jax.ShapeDtypeStruct((n, LANES), BF16)]
        out_specs += [rows(LANES)]
    return pl.pallas_call(
        functools.partial(_mla_prep_kernel, expand=expand),
        grid=(n // tm,),
        in_specs=in_specs,
        out_specs=out_specs,
        out_shape=out_shape,
        compiler_params=pltpu.CompilerParams(dimension_semantics=("arbitrary",),
                                             vmem_limit_bytes=VMEM_LIMIT),
        name="mla_prep",
    )(*in_arrays)


def _flash_kernel(qn_ref, qr_ref, kn_ref, kr_ref, vt_ref, o_ref, s0_ref, s1_ref, p0_ref, p1_ref, bm0_ref, bm1_ref,
                  a0_ref, a1_ref, m_ref, acc_ref, *, t):
    s_ref, p_ref, bm_ref, a_ref = (s0_ref, s1_ref), (p0_ref, p1_ref), (bm0_ref, bm1_ref), (a0_ref, a1_ref)
    n = pl.program_id(2) + 1
    m_ref[...] = jnp.full(m_ref.shape, -jnp.inf, F32)
    acc_ref[...] = jnp.zeros(acc_ref.shape, F32)

    def scores(j, slot, masked=False):
        start = pl.multiple_of(j * t, t)
        k = jnp.concatenate([kn_ref[0, pl.ds(start, t), :], kr_ref[0, pl.ds(start, t), :]], axis=1)
        q = jnp.concatenate([qn_ref[0], qr_ref[0]], axis=1)
        st = _dot_nt(k, q)
        if masked:
            key = lax.broadcasted_iota(jnp.int32, st.shape, 0)
            qry = lax.broadcasted_iota(jnp.int32, st.shape, 1)
            st = jnp.where((key // CHUNK) <= (qry // CHUNK), st, NEG_INF)
        s_ref[slot][...] = st
        bm_ref[slot][...] = jnp.max(st, axis=0, keepdims=True)

    def softmax(slot):
        m_old = m_ref[...]
        m_new = jnp.maximum(m_old, bm_ref[slot][...])
        alpha = jnp.exp2(m_old - m_new)
        p_ref[slot][...] = jnp.exp2(s_ref[slot][...] - m_new).astype(BF16)
        a_ref[slot][...] = alpha
        m_ref[...] = m_new

    def values(j, slot):
        acc_ref[...] = a_ref[slot][...] * acc_ref[...] + _dot(vt_ref[0, 0, j], p_ref[slot][...])

    @pl.when(n <= 2)
    def _():
        @pl.when(n == 2)
        def _():
            scores(0, 0)
            softmax(0)
            values(0, 0)
        scores(n - 1, 1, masked=True)
        softmax(1)
        values(n - 1, 1)

    @pl.when(n >= 3)
    def _():
        scores(0, 0)
        scores(1, 1)
        softmax(0)

    def pair(c, carry):
        i = 2 * c + 1
        scores(i + 1, 0)
        softmax(1)
        values(i - 1, 0)
        scores(i + 2, 1)
        softmax(0)
        values(i, 1)
        return carry

    lax.fori_loop(0, jnp.maximum(n - 3, 0) // 2, pair, 0)

    @pl.when((n >= 3) & (n % 2 == 1))
    def _():
        scores(n - 1, 0, masked=True)
        softmax(1)
        values(n - 3, 0)
        softmax(0)
        values(n - 2, 1)
        values(n - 1, 0)

    @pl.when((n >= 3) & (n % 2 == 0))
    def _():
        scores(n - 2, 0)
        softmax(1)
        values(n - 4, 0)
        scores(n - 1, 1, masked=True)
        softmax(0)
        values(n - 3, 1)
        softmax(1)
        values(n - 2, 0)
        values(n - 1, 1)

    o_ref[0] = (acc_ref[:V_DIM, :] / acc_ref[V_DIM:V_DIM + 1, :]).T.astype(o_ref.dtype)


def _flash_attention(qn, qr, kn, kr2, vt, *, t):
    b, s, _ = qn.shape
    n_kb = s // t
    return pl.pallas_call(
        functools.partial(_flash_kernel, t=t),
        grid=(b, MLA_HEADS, s // t),
        in_specs=[
            pl.BlockSpec((1, t, LANES), lambda bi, h, qi: (bi, qi, h)),
            pl.BlockSpec((1, t, LANES), lambda bi, h, qi: (bi, qi, h // 2)),
            pl.BlockSpec((1, s, LANES), lambda bi, h, qi: (bi, 0, h)),
            pl.BlockSpec((1, s, LANES), lambda bi, h, qi: (bi, 0, h % 2)),
            pl.BlockSpec((1, 1, n_kb, V_DIM + V_PAD, t), lambda bi, h, qi: (bi, h, 0, 0, 0)),
        ],
        out_specs=pl.BlockSpec((1, t, LANES), lambda bi, h, qi: (bi, qi, h)),
        out_shape=jax.ShapeDtypeStruct((b, s, MLA_HEADS * V_DIM), BF16),
        scratch_shapes=[pltpu.VMEM((t, t), F32), pltpu.VMEM((t, t), F32),
                        pltpu.VMEM((t, t), BF16), pltpu.VMEM((t, t), BF16),
                        pltpu.VMEM((1, t), F32), pltpu.VMEM((1, t), F32),
                        pltpu.VMEM((1, t), F32), pltpu.VMEM((1, t), F32),
                        pltpu.VMEM((1, t), F32), pltpu.VMEM((V_DIM + V_PAD, t), F32)],
        compiler_params=pltpu.CompilerParams(dimension_semantics=("arbitrary", "arbitrary", "arbitrary"),
                                             vmem_limit_bytes=VMEM_LIMIT),
        name="flash_attention",
    )(qn, qr, kn, kr2, vt)


def _sample_attn_kernel(qn_ref, qr_ref, ckvn_ref, kdup_ref, cache_ref, krc_ref, wuk_ref, wuv_ref, o_ref,
                        *, past, n_new, need_mask, key_chunk):
    qn = qn_ref[0]
    qr = qr_ref[0]
    lane = lax.broadcasted_iota(jnp.int32, (n_new, LANES), 1)
    q_lat, q_rope = [], []
    for h in range(MLA_HEADS):
        q_lat.append(_dot_nt(qn[:, h * NOPE_DIM:(h + 1) * NOPE_DIM],
                             wuk_ref[:, h * NOPE_DIM:(h + 1) * NOPE_DIM]).astype(BF16))
        pair = qr[:, (h // 2) * LANES:(h // 2 + 1) * LANES]
        half = (lane < ROPE_DIM) if h % 2 == 0 else (lane >= ROPE_DIM)
        q_rope.append(jnp.where(half, pair, jnp.zeros_like(pair)))
    q_all = jnp.concatenate([jnp.concatenate(q_lat, axis=0), jnp.concatenate(q_rope, axis=0)], axis=1)

    rows = MLA_HEADS * n_new
    q_pos = past + lax.broadcasted_iota(jnp.int32, (rows, 1), 0) % n_new

    def chunk_keys(c):
        if c < past // key_chunk:
            ks = slice(c * key_chunk, (c + 1) * key_chunk)
            kr = krc_ref[0, 0, ks, :].astype(BF16)
            return (jnp.concatenate([cache_ref[0, 0, ks, :].astype(BF16), kr, kr], axis=1),
                    c * key_chunk, key_chunk)
        pad = LANES - n_new
        c_new = jnp.concatenate([ckvn_ref[0].astype(BF16), jnp.zeros((pad, KV_LORA), BF16)], axis=0)
        r_new = jnp.concatenate([kdup_ref[0], jnp.zeros((pad, LANES), BF16)], axis=0)
        return jnp.concatenate([c_new, r_new], axis=1), past, n_new

    m = jnp.full((rows, 1), -jnp.inf, F32)
    denom = jnp.zeros((rows, 1), F32)
    acc = jnp.zeros((rows, KV_LORA), F32)
    for c in range(past // key_chunk + 1):
        keys, pos0, n_real = chunk_keys(c)
        s = _dot_nt(q_all, keys)
        col = lax.broadcasted_iota(jnp.int32, s.shape, 1)
        visible = None
        if n_real < keys.shape[0]:
            visible = col < n_real
        if need_mask:
            causal = ((pos0 + col) // CHUNK) <= (q_pos // CHUNK)
            visible = causal if visible is None else (visible & causal)
        if visible is not None:
            s = jnp.where(visible, s, NEG_INF)
        m_new = jnp.maximum(m, jnp.max(s, axis=1, keepdims=True))
        alpha = jnp.exp(m - m_new)
        p = jnp.exp(s - m_new)
        denom = alpha * denom + jnp.sum(p, axis=1, keepdims=True)
        acc = alpha * acc + _dot(p.astype(BF16), keys[:, :KV_LORA])
        m = m_new
    o_lat = (acc / denom).astype(BF16)
    outs = [_dot(o_lat[h * n_new:(h + 1) * n_new], wuv_ref[:, h * V_DIM:(h + 1) * V_DIM])
            for h in range(MLA_HEADS)]
    o_ref[0] = jnp.concatenate(outs, axis=1).astype(o_ref.dtype)


def _sample_attention(qn, qr, ckv_new, kdup, cache_ckv, cache_krope, wuk, wuv, *, need_mask):
    nb, n_new, _ = qn.shape
    past = cache_ckv.shape[2]
    key_chunk = SAMPLE_KEY_CHUNK if past % SAMPLE_KEY_CHUNK == 0 else past
    return pl.pallas_call(
        functools.partial(_sample_attn_kernel, past=past, n_new=n_new, need_mask=need_mask, key_chunk=key_chunk),
        grid=(nb,),
        in_specs=[
            pl.BlockSpec((1, n_new, qn.shape[2]), lambda i: (i, 0, 0)),
            pl.BlockSpec((1, n_new, qr.shape[2]), lambda i: (i, 0, 0)),
            pl.BlockSpec((1, n_new, KV_LORA), lambda i: (i, 0, 0)),
            pl.BlockSpec((1, n_new, LANES), lambda i: (i, 0, 0)),
            pl.BlockSpec((1, 1, past, KV_LORA), lambda i: (0, i, 0, 0)),
            pl.BlockSpec((1, 1, past, ROPE_DIM), lambda i: (0, i, 0, 0)),
            _resident(wuk.shape), _resident(wuv.shape),
        ],
        out_specs=pl.BlockSpec((1, n_new, MLA_HEADS * V_DIM), lambda i: (i, 0, 0)),
        out_shape=jax.ShapeDtypeStruct((nb, n_new, MLA_HEADS * V_DIM), BF16),
        compiler_params=pltpu.CompilerParams(dimension_semantics=("arbitrary",),
                                             vmem_limit_bytes=VMEM_LIMIT),
        name="sample_attention",
    )(qn, qr, ckv_new, kdup, cache_ckv, cache_krope, wuk, wuv)


def _out_ffn_kernel(x_ref, ga_ref, gb_ref, attn_ref, prev_ref, wb_ref, wo_ref, fg_ref, wup_ref, cw_ref, cb_ref,
                    wdn_ref, fing_ref, y_ref, conv_ref, carry_ref, *, seg_rows, tiles_per_seq):
    tm = x_ref.shape[0]
    nseg = tm // seg_rows
    merged = ga_ref[...].astype(F32) + gb_ref[...].astype(F32) * _dot(attn_ref[...], wb_ref[...])
    h = x_ref[...] + _dot(merged.astype(BF16), wo_ref[...])
    hn = _rms(h, fg_ref[...]).astype(BF16)

    if tiles_per_seq == 1:
        prev = prev_ref[...]
    else:
        first = (pl.program_id(0) % tiles_per_seq) == 0
        prev = jnp.where(first, prev_ref[...], carry_ref[...][None])

    row = lax.broadcasted_iota(jnp.int32, (SUBLANES, FF_CHUNK), 0)

    def shifted(u, p0, p1):
        r1 = pltpu.roll(u, 1, 0)
        r2 = pltpu.roll(u, 2, 0)
        top1 = jnp.where(row == 0, p1, r1[:SUBLANES])
        top2 = jnp.where(row == 0, p0, jnp.where(row == 1, p1, r2[:SUBLANES]))
        return (jnp.concatenate([top1, r1[SUBLANES:]], axis=0), jnp.concatenate([top2, r2[SUBLANES:]], axis=0))

    def conv(up, c0):
        cols = slice(c0, c0 + FF_CHUNK)
        w = cw_ref[:, cols]
        outs = []
        for s in range(nseg):
            u = up[s * seg_rows:(s + 1) * seg_rows]
            d1, d2 = shifted(u, prev[s, 0:1, cols], prev[s, 1:2, cols])
            outs.append(cb_ref[:, cols] + d2 * w[0:1] + d1 * w[1:2] + u * w[2:3])
            tail = u[seg_rows - (CONV_W - 1):]
            conv_ref[s, :, cols] = tail
            if tiles_per_seq > 1:
                carry_ref[:, cols] = tail
        return outs[0] if nseg == 1 else jnp.concatenate(outs, axis=0)

    acts = []
    for j in range(N_FF_CHUNKS):
        c_val = j * FF_CHUNK
        c_gate = D_FF + j * FF_CHUNK
        val = conv(_dot(hn, wup_ref[:, c_val:c_val + FF_CHUNK]), c_val)
        gate = conv(_dot(hn, wup_ref[:, c_gate:c_gate + FF_CHUNK]), c_gate)
        acts.append((jax.nn.silu(gate) * val).astype(BF16))
    ffn = _dot(jnp.concatenate(acts, axis=1), wdn_ref[...])
    y_ref[...] = _rms(h + ffn, fing_ref[...])


def _out_ffn(x2d, ga, gb, attn, prev, wb, wo, ffn_g, wup, conv_w, conv_b, wdn, fin_g, *, seg_rows, tm):
    n = x2d.shape[0]
    nseg = tm // seg_rows if seg_rows < tm else 1
    seg = min(seg_rows, tm)
    tiles_per_seq = max(seg_rows // tm, 1)
    row_spec = pl.BlockSpec((tm, D_MODEL), lambda i: (i, 0))
    state_spec = pl.BlockSpec((nseg, CONV_W - 1, 2 * D_FF),
                              (lambda i: (i // tiles_per_seq, 0, 0)) if tiles_per_seq > 1 else (lambda i: (i, 0, 0)))
    return pl.pallas_call(
        functools.partial(_out_ffn_kernel, seg_rows=seg, tiles_per_seq=tiles_per_seq),
        grid=(n // tm,),
        in_specs=[row_spec, row_spec, row_spec, row_spec, state_spec,
                  _resident(wb.shape), _resident(wo.shape), _resident(ffn_g.shape), _resident(wup.shape),
                  _resident(conv_w.shape), _resident(conv_b.shape), _resident(wdn.shape),
                  _resident(fin_g.shape)],
        out_specs=[row_spec, state_spec],
        out_shape=[jax.ShapeDtypeStruct((n, D_MODEL), F32),
                   jax.ShapeDtypeStruct(prev.shape, F32)],
        scratch_shapes=[pltpu.VMEM((CONV_W - 1, 2 * D_FF), F32)],
        compiler_params=pltpu.CompilerParams(dimension_semantics=("arbitrary",),
                                             vmem_limit_bytes=VMEM_LIMIT),
        name="out_ffn",
    )(x2d, ga, gb, attn, prev, wb, wo, ffn_g, wup, conv_w, conv_b, wdn, fin_g)


def _rope_tables(pos):
    inv_freq = ROPE_BASE ** (-jnp.arange(HALF_ROPE, dtype=F32) / HALF_ROPE)
    ang = pos.astype(F32)[:, None] * inv_freq[None, :]
    reps = LANES // HALF_ROPE
    return jnp.tile(jnp.cos(ang), (1, reps)), jnp.tile(jnp.sin(ang), (1, reps))


def _rotated_cols(w):
    return jnp.concatenate([-w[..., HALF_ROPE:], w[..., :HALF_ROPE]], axis=-1)


def kernel(x_prompt, x_sample, cache_mla_ckv, cache_mla_krope, state_ffn_conv, norm_mix_g, w_in, gmlp_ln_g,
           gmlp_ln_b, gmlp_w_s, gmlp_b_s, mla_q_norm_g, mla_w_uq, mla_kv_norm_g, mla_w_uk, mla_w_uv,
           w_proj_a, w_proj_b, w_out, norm_ffn_g, ffn_w_up, ffn_conv_w, ffn_conv_b, ffn_w_down,
           final_norm_g):
    assert w_in.shape[0] == 1, "single-layer step"
    bp, sp, _ = x_prompt.shape
    bs, ss, _ = x_sample.shape
    past = cache_mla_ckv.shape[2]
    assert sp % GMLP_CHUNK == 0 and ss <= GMLP_CHUNK and GMLP_CHUNK % ss == 0

    w0 = w_in[0]
    wuv_in = w0[:, :OFF_Q].astype(BF16)
    wgate = w0[:, OFF_GATE:].astype(BF16)
    w_kr = w0[:, OFF_KV + KV_LORA:OFF_GATE]
    w_kr_rot = _rotated_cols(w_kr)
    wmla = jnp.concatenate([w0[:, OFF_Q:OFF_KV + KV_LORA], w_kr, w_kr, w_kr_rot, w_kr_rot], axis=1).astype(BF16)
    uq = mla_w_uq[0]
    uq_rope = uq[:, :, NOPE_DIM:]
    wuq = jnp.concatenate([uq[:, :, :NOPE_DIM].reshape(Q_LORA, -1), uq_rope.reshape(Q_LORA, -1),
                           _rotated_cols(uq_rope).reshape(Q_LORA, -1)], axis=1).astype(BF16)
    wuk = mla_w_uk[0].reshape(KV_LORA, -1).astype(BF16)
    wuv = mla_w_uv[0].reshape(KV_LORA, -1).astype(BF16)
    wa = w_proj_a[0].astype(BF16)
    wb = w_proj_b[0].astype(BF16)
    wo = w_out[0].astype(BF16)
    wup = ffn_w_up[0].astype(BF16)
    wdn = ffn_w_down[0].astype(BF16)
    conv_w = ffn_conv_w[0]
    conv_b = ffn_conv_b

    def spatial_params(n):
        reps = GMLP_CHUNK // n
        ws = jnp.tile(gmlp_w_s[0][:, :n, :n], (1, reps, reps))
        bias = jnp.tile(jnp.repeat(gmlp_b_s[0][:, :n].T, GMLP_GROUP_DIM, axis=1), (reps, 1))
        return ws, bias

    def run_stream(x, pos_row, n_spatial, prev_conv, *, is_prompt):
        b, s, _ = x.shape
        x2d = x.reshape(b * s, D_MODEL)
        ws, bias = spatial_params(n_spatial)
        res = _gmlp_gate(x2d, norm_mix_g, wuv_in, wgate, gmlp_ln_g, gmlp_ln_b, ws, bias, wa,
                         seg=n_spatial, want_v=not is_prompt, tm=PREP_ROWS)
        cos_t, sin_t = _rope_tables(pos_row)
        prep = _mla_prep(x2d, norm_mix_g, wmla, mla_q_norm_g, wuq, mla_kv_norm_g, cos_t, sin_t, wuk,
                         wuv.T if is_prompt else wuv, expand=is_prompt, tm=PREP_ROWS, seq=s, kv_tile=FLASH_ROWS)
        if is_prompt:
            ga, gb = res
            qn, qr, ckv, kro, kn, kr2, vt = prep
            r3 = lambda a: a.reshape(b, s, a.shape[-1])
            attn = _flash_attention(r3(qn), r3(qr), r3(kn), r3(kr2), vt, t=FLASH_ROWS)
        else:
            ga, gb, v_rows = res
            qn, qr, ckv, kro, kdup = prep
            r3 = lambda a: a.reshape(b, s, a.shape[-1])
            k_pos = np.arange(past + s)
            q_pos = past + np.arange(s)
            need_mask = not bool(np.all((k_pos[None, :] // CHUNK) <= (q_pos[:, None] // CHUNK)))
            attn = _sample_attention(r3(qn), r3(qr), r3(ckv), r3(kdup), cache_mla_ckv, cache_mla_krope, wuk, wuv,
                                     need_mask=need_mask)
        y, conv_rows = _out_ffn(x2d, ga, gb, attn.reshape(b * s, -1), prev_conv, wb, wo, norm_ffn_g, wup,
                                conv_w, conv_b, wdn, final_norm_g[None, :], seg_rows=s, tm=FFN_ROWS)
        outs = (y.reshape(b, s, D_MODEL), ckv.reshape(1, b, s, KV_LORA), kro.reshape(1, b, s, ROPE_DIM),
                conv_rows[None])
        if not is_prompt:
            outs = outs + (v_rows.reshape(1, b, s, GMLP_WIDTH),)
        return outs

    pos_p = jnp.arange(sp, dtype=jnp.int32)
    pos_s = jnp.tile(past + jnp.arange(ss, dtype=jnp.int32), (max(PREP_ROWS // ss, 1),))
    conv_zero = jnp.zeros((bp, CONV_W - 1, 2 * D_FF), x_prompt.dtype)

    y_p, ckv_p, kr_p, cv_p = run_stream(x_prompt, pos_p, GMLP_CHUNK, conv_zero, is_prompt=True)
    y_s, ckv_s, kr_s, cv_s, gv_s = run_stream(x_sample, pos_s, ss, state_ffn_conv[0], is_prompt=False)
    return (y_p, y_s, ckv_p, kr_p, cv_p, ckv_s, kr_s, cv_s, gv_s)
```
